```python
import math, functools
import jax, jax.numpy as jnp
from jax import lax
import numpy as np

D_MODEL = 1024
BATCH = 8
SEQ = 8192
DEPTH = 1

N_META = 16
CHUNK = 64
GDN_HEADS = 8
GDN_DK = 128
GDN_DV = 128
RET_HEADS = 8
RET_DK = 128
RET_DV = 128
CONV_K = 4
D_FF = 2816
ROPE_BASE = 10000.0
EPS = 1e-6

GDN_QK = GDN_HEADS * GDN_DK
GDN_V = GDN_HEADS * GDN_DV
GDN_CONV = 2 * GDN_QK + GDN_V
RET_QK = RET_HEADS * RET_DK
RET_V = RET_HEADS * RET_DV
PROJ_SIZES = (GDN_CONV, GDN_V, GDN_HEADS, GDN_HEADS, RET_QK, RET_QK, RET_V, RET_V, D_MODEL, D_MODEL)
D_PROJ = sum(PROJ_SIZES)

kernel_name = "hybrid_gdn_retention_macaron_layer"


def rms_norm(x, w):
    xf = x.astype(jnp.float32)
    y = xf * lax.rsqrt(jnp.mean(xf * xf, axis=-1, keepdims=True) + EPS)
    return (y * w.astype(jnp.float32)).astype(x.dtype)


def swiglu(x, w_in, w_out):
    gate, up = jnp.split(x @ w_in, 2, axis=-1)
    return (jax.nn.silu(gate) * up) @ w_out


def causal_depthwise_conv(x, w):
    c = x.shape[-1]
    return lax.conv_general_dilated(
        x, w[:, None, :].astype(x.dtype), window_strides=(1,), padding=[(CONV_K - 1, 0)],
        dimension_numbers=("NWC", "WIO", "NWC"), feature_group_count=c)


def to_heads(t, n_heads):
    b, l, _ = t.shape
    return t.reshape(b, l, n_heads, -1).transpose(0, 2, 1, 3).astype(jnp.float32)


def l2norm(t):
    return t * lax.rsqrt(jnp.sum(t * t, axis=-1, keepdims=True) + EPS)


def rotary(t, pos):
    d = t.shape[-1]
    inv = 1.0 / (ROPE_BASE ** jnp.linspace(0.0, 1.0, d // 2, dtype=jnp.float32))
    ang = pos[:, None] * inv[None, :]
    cos, sin = jnp.cos(ang), jnp.sin(ang)
    tp = t.reshape(*t.shape[:-1], d // 2, 2)
    t0, t1 = tp[..., 0], tp[..., 1]
    return jnp.stack([t0 * cos - t1 * sin, t1 * cos + t0 * sin], axis=-1).reshape(t.shape)


def gdn_chunk_scan(q, k, v, g, beta, state, chunk):
    b, h, l, dk = q.shape
    n = l // chunk
    split = lambda t: t.reshape(b, h, n, chunk, *t.shape[3:])
    q, k, v, g, beta = split(q), split(k), split(v), split(g), split(beta)
    g = jnp.cumsum(g, axis=-1)
    causal = jnp.tril(jnp.ones((chunk, chunk), dtype=bool))
    strict = jnp.tril(jnp.ones((chunk, chunk), dtype=bool), -1)
    diff = g[..., :, None] - g[..., None, :]
    decay = jnp.where(causal, jnp.exp(jnp.where(causal, diff, 0.0)), 0.0)
    k_beta = k * beta[..., None]
    a = jnp.where(strict, jnp.einsum("bhncd,bhnmd->bhncm", k_beta, k) * decay, 0.0) + jnp.eye(chunk, dtype=q.dtype)
    solve = functools.partial(lax.linalg.triangular_solve, left_side=True, lower=True)
    u = solve(a, v * beta[..., None])
    w = solve(a, k_beta * jnp.exp(g)[..., None])
    qk = jnp.einsum("bhncd,bhnmd->bhncm", q, k) * decay
    g_last = g[..., -1]
    q_dec = q * jnp.exp(g)[..., None]
    k_dec = k * jnp.exp(g_last[..., None] - g)[..., None]

    def step(s, xs):
        qk_c, u_c, w_c, qd_c, kd_c, gl_c = xs
        v_new = u_c - jnp.einsum("bhck,bhkv->bhcv", w_c, s)
        o = jnp.einsum("bhck,bhkv->bhcv", qd_c, s) + jnp.einsum("bhcm,bhmv->bhcv", qk_c, v_new)
        s = s * jnp.exp(gl_c)[..., None, None] + jnp.einsum("bhck,bhcv->bhkv", kd_c, v_new)
        return s, o

    xs = tuple(jnp.moveaxis(t, 2, 0) for t in (qk, u, w, q_dec, k_dec, g_last))
    state, o = lax.scan(step, state, xs)
    return jnp.moveaxis(o, 0, 2).reshape(b, h, l, -1), state


def retention_chunk_scan(q, k, v, log_gamma, state, chunk):
    b, h, l, dk = q.shape
    n = l // chunk
    split = lambda t: t.reshape(b, h, n, chunk, t.shape[-1])
    q, k, v = split(q), split(k), split(v)
    pos = jnp.arange(chunk, dtype=jnp.float32)
    lg = log_gamma[:, None]
    causal = jnp.tril(jnp.ones((chunk, chunk), dtype=bool))
    diff = pos[:, None] - pos[None, :]
    decay = jnp.where(causal, jnp.exp(jnp.where(causal, diff, 0.0) * log_gamma[:, None, None]), 0.0)
    scores = jnp.einsum("bhncd,bhnmd->bhncm", q, k) * decay[None, :, None]
    intra = jnp.einsum("bhncm,bhnmv->bhncv", scores, v)
    q_dec = q * jnp.exp((pos + 1.0) * lg)[None, :, None, :, None]
    k_dec = k * jnp.exp((chunk - 1.0 - pos) * lg)[None, :, None, :, None]
    chunk_decay = jnp.exp(chunk * log_gamma)[None, :, None, None]

    def step(s, xs):
        qd, kd, vc = xs
        o = jnp.einsum("bhck,bhkv->bhcv", qd, s)
        s = s * chunk_decay + jnp.einsum("bhck,bhcv->bhkv", kd, vc)
        return s, o

    xs = tuple(jnp.moveaxis(t, 2, 0) for t in (q_dec, k_dec, v))
    state, inter = lax.scan(step, state, xs)
    o = intra + jnp.moveaxis(inter, 0, 2)
    return o.reshape(b, h, l, -1), state


def hybrid_mixer(n, w_in, conv_w, a_log, dt_bias, gdn_norm, ret_norm, w_br_gdn, w_br_ret, w_out):
    b, l, _ = n.shape
    f32 = jnp.float32
    offs = [int(i) for i in np.cumsum(PROJ_SIZES)[:-1]]
    qkv, z, b_raw, a_raw, rq, rk, rv, rg, ga, gb = jnp.split(n @ w_in, offs, axis=-1)

    qkv = jax.nn.silu(causal_depthwise_conv(qkv, conv_w))
    q, k, v = jnp.split(qkv, [GDN_QK, 2 * GDN_QK], axis=-1)
    q = l2norm(to_heads(q, GDN_HEADS)) * (GDN_DK ** -0.5)
    k = l2norm(to_heads(k, GDN_HEADS))
    v = to_heads(v, GDN_HEADS)
    g = (-jnp.exp(a_log.astype(f32)) * jax.nn.softplus(a_raw.astype(f32) + dt_bias.astype(f32))).transpose(0, 2, 1)
    beta = jax.nn.sigmoid(b_raw.astype(f32)).transpose(0, 2, 1)
    s0 = jnp.zeros((b, GDN_HEADS, GDN_DK, GDN_DV), f32)
    o_m, s_m = gdn_chunk_scan(q[:, :, :N_META], k[:, :, :N_META], v[:, :, :N_META],
                              g[:, :, :N_META], beta[:, :, :N_META], s0, N_META)
    o_r, _ = gdn_chunk_scan(q[:, :, N_META:], k[:, :, N_META:], v[:, :, N_META:],
                            g[:, :, N_META:], beta[:, :, N_META:], s_m, CHUNK)
    o_a = jnp.concatenate([o_m, o_r], axis=2).transpose(0, 2, 1, 3)
    o_a = o_a * lax.rsqrt(jnp.mean(o_a * o_a, axis=-1, keepdims=True) + EPS) * gdn_norm.astype(f32)
    y_a = (o_a * jax.nn.silu(z.reshape(b, l, GDN_HEADS, GDN_DV).astype(f32))).reshape(b, l, GDN_V).astype(n.dtype)

    pos = jnp.arange(l, dtype=f32)
    rq = rotary(to_heads(rq, RET_HEADS), pos)
    rk = rotary(to_heads(rk, RET_HEADS), pos) * (RET_DK ** -0.5)
    rv = to_heads(rv, RET_HEADS)
    log_gamma = jnp.log1p(-jnp.exp2(-5.0 - jnp.arange(RET_HEADS, dtype=f32)))
    r0 = jnp.zeros((b, RET_HEADS, RET_DK, RET_DV), f32)
    p_m, r_m = retention_chunk_scan(rq[:, :, :N_META], rk[:, :, :N_META], rv[:, :, :N_META], log_gamma, r0, N_META)
    p_r, _ = retention_chunk_scan(rq[:, :, N_META:], rk[:, :, N_META:], rv[:, :, N_META:], log_gamma, r_m, CHUNK)
    o_b = jnp.concatenate([p_m, p_r], axis=2).transpose(0, 2, 1, 3)
    mu = jnp.mean(o_b, axis=-1, keepdims=True)
    var = jnp.mean(jnp.square(o_b - mu), axis=-1, keepdims=True)
    o_b = ((o_b - mu) * lax.rsqrt(var + EPS)).reshape(b, l, RET_V) * ret_norm.astype(f32)
    y_b = (jax.nn.silu(rg.astype(f32)) * o_b).astype(n.dtype)

    merged = jax.nn.sigmoid(ga) * (y_a @ w_br_gdn) + jax.nn.sigmoid(gb) * (y_b @ w_br_ret)
    return merged @ w_out


def setup_inputs(seed: int = 0) -> dict:
    key = jax.random.key(seed)
    ks = jax.random.split(key, 20)
    f32 = jnp.float32
    nrm = lambda k, shape, scale: jax.random.normal(k, shape, f32) * scale
    gain = lambda k, shape: 1.0 + 0.02 * jax.random.normal(k, shape, f32)
    dt = jnp.exp(jax.random.uniform(ks[8], (DEPTH, GDN_HEADS), f32, math.log(1e-3), math.log(1e-1)))
    return {
        "x": nrm(ks[0], (BATCH, SEQ, D_MODEL), 1.0),
        "meta_tokens": nrm(ks[1], (N_META, D_MODEL), 1.0),
        "ffn1_norm": gain(ks[2], (DEPTH, D_MODEL)),
        "ffn1_w_in": nrm(ks[3], (DEPTH, D_MODEL, 2 * D_FF), D_MODEL ** -0.5),
        "ffn1_w_out": nrm(ks[4], (DEPTH, D_FF, D_MODEL), D_FF ** -0.5),
        "mix_norm": gain(ks[5], (DEPTH, D_MODEL)),
        "w_in": nrm(ks[6], (DEPTH, D_MODEL, D_PROJ), D_MODEL ** -0.5),
        "gdn_conv_w": nrm(ks[7], (DEPTH, CONV_K, GDN_CONV), CONV_K ** -0.5),
        "gdn_a_log": jnp.log(jax.random.uniform(ks[9], (DEPTH, GDN_HEADS), f32, 1.0, 16.0)),
        "gdn_dt_bias": dt + jnp.log(-jnp.expm1(-dt)),
        "gdn_out_norm": gain(ks[10], (DEPTH, GDN_DV)),
        "ret_out_norm": gain(ks[11], (DEPTH, RET_V)),
        "w_branch_gdn": nrm(ks[12], (DEPTH, GDN_V, D_MODEL), GDN_V ** -0.5),
        "w_branch_ret": nrm(ks[13], (DEPTH, RET_V, D_MODEL), RET_V ** -0.5),
        "w_out": nrm(ks[14], (DEPTH, D_MODEL, D_MODEL), D_MODEL ** -0.5),
        "ffn2_norm": gain(ks[15], (DEPTH, D_MODEL)),
        "ffn2_w_in": nrm(ks[16], (DEPTH, D_MODEL, 2 * D_FF), D_MODEL ** -0.5),
        "ffn2_w_out": nrm(ks[17], (DEPTH, D_FF, D_MODEL), D_FF ** -0.5),
        "final_norm": gain(ks[18], (D_MODEL,)),
    }


def reference(x, meta_tokens, ffn1_norm, ffn1_w_in, ffn1_w_out, mix_norm, w_in, gdn_conv_w,
              gdn_a_log, gdn_dt_bias, gdn_out_norm, ret_out_norm, w_branch_gdn, w_branch_ret,
              w_out, ffn2_norm, ffn2_w_in, ffn2_w_out, final_norm):
    b = x.shape[0]
    meta = jnp.broadcast_to(meta_tokens[None].astype(x.dtype), (b, N_META, D_MODEL))
    h = jnp.concatenate([meta, x], axis=1)
    for i in range(DEPTH):
        h = h + 0.5 * swiglu(rms_norm(h, ffn1_norm[i]), ffn1_w_in[i], ffn1_w_out[i])
        h = h + hybrid_mixer(rms_norm(h, mix_norm[i]), w_in[i], gdn_conv_w[i], gdn_a_log[i],
                             gdn_dt_bias[i], gdn_out_norm[i], ret_out_norm[i],
                             w_branch_gdn[i], w_branch_ret[i], w_out[i])
        h = h + 0.5 * swiglu(rms_norm(h, ffn2_norm[i]), ffn2_w_in[i], ffn2_w_out[i])
    return rms_norm(h, final_norm)[:, N_META:]
```

```python
import functools
import math

import jax
import jax.numpy as jnp
from jax import lax
from jax.experimental import pallas as pl
from jax.experimental.pallas import tpu as pltpu

F32 = jnp.float32
BF16 = jnp.bfloat16

EPS = 1e-6
N_META = 16
HEADS = 8
HEAD_DIM = 128
CONV_K = 4
ROPE_BASE = 10000.0
RET_LOG_GAMMA = tuple(math.log1p(-(2.0 ** (-5.0 - h))) for h in range(HEADS))

LANES = 128
SUBLANES = 8
V7X_VMEM_BYTES = 64 * 1024 * 1024

SCAN_CHUNK = LANES
MIXER_ROWS = 2 * SCAN_CHUNK
FFN_ROWS = 512
FFN_COLS = 256
ROPE_ROWS = 256


def _dot(a, b):
    return jnp.dot(a.astype(BF16), b.astype(BF16), preferred_element_type=F32)


def _dot_nt(a, b):
    return lax.dot_general(a.astype(BF16), b.astype(BF16), (((1,), (1,)), ((), ())),
                           preferred_element_type=F32)


def _dot_tn(a, b):
    return lax.dot_general(a.astype(BF16), b.astype(BF16), (((0,), (0,)), ((), ())),
                           preferred_element_type=F32)


def _split2(a):
    hi = a.astype(BF16)
    lo = (a - hi.astype(F32)).astype(BF16)
    return hi, lo


def _dot_split(a, b):
    ah, al = _split2(a)
    bh, bl = _split2(b)
    d = functools.partial(jnp.dot, preferred_element_type=F32)
    return d(ah, bh) + (d(ah, bl) + d(al, bh))


def _sigmoid(x):
    return 1.0 / (1.0 + jnp.exp(-x))


def _softplus(x):
    return jnp.maximum(x, 0.0) + jnp.log1p(jnp.exp(-jnp.abs(x)))


def _rms(x, w):
    return x * lax.rsqrt(jnp.mean(x * x, axis=-1, keepdims=True) + EPS) * w


def _rope_body(inv_ref, cos_ref, sin_ref, *, pos0):
    rows = cos_ref.shape[0]
    row = lax.broadcasted_iota(jnp.int32, (rows, LANES), 0)
    lane = lax.broadcasted_iota(jnp.int32, (rows, LANES), 1)
    pos = (row + (pl.program_id(0) * rows + pos0)).astype(F32)
    ang = pos * inv_ref[...]
    cos_ref[...] = jnp.cos(ang)
    sin_ref[...] = jnp.where(lane < LANES // 2, -1.0, 1.0) * jnp.sin(ang)


def _rope_tables(inv2, n_rows, pos0):
    rows = min(ROPE_ROWS, n_rows)
    assert n_rows % rows == 0
    out = jax.ShapeDtypeStruct((n_rows, LANES), F32)
    spec = pl.BlockSpec((rows, LANES), lambda i: (i, 0))
    return pl.pallas_call(
        functools.partial(_rope_body, pos0=pos0),
        grid=(n_rows // rows,),
        in_specs=[pl.BlockSpec((1, LANES), lambda i: (0, 0))],
        out_specs=[spec, spec],
        out_shape=[out, out],
        name="rope_tables",
    )(inv2)


def _ffn_body(x_ref, nw_ref, win_ref, wout_ref, fw_ref, o_ref, *, final):
    d_ff = wout_ref.shape[0]
    x = x_ref[...]
    n = _rms(x, nw_ref[...]).astype(BF16)
    acc = jnp.zeros(x.shape, F32)
    for f in range(0, d_ff, FFN_COLS):
        gate = jnp.dot(n, win_ref[:, f:f + FFN_COLS], preferred_element_type=F32)
        up = jnp.dot(n, win_ref[:, d_ff + f:d_ff + f + FFN_COLS], preferred_element_type=F32)
        act = (gate * _sigmoid(gate) * up).astype(BF16)
        acc = acc + jnp.dot(act, wout_ref[f:f + FFN_COLS, :], preferred_element_type=F32)
    h = x + 0.5 * acc
    if final:
        h = _rms(h, fw_ref[...])
    o_ref[...] = h


def _resident(shape):
    return pl.BlockSpec(shape, lambda *_: (0,) * len(shape), pipeline_mode=pl.Buffered(1))


def _ffn(x2, norm_w, w_in, w_out, final_w, *, final):
    rows, d = x2.shape
    d_ff = w_out.shape[0]
    assert d_ff % FFN_COLS == 0
    tm = min(FFN_ROWS, rows)
    assert rows % tm == 0
    return pl.pallas_call(
        functools.partial(_ffn_body, final=final),
        grid=(rows // tm,),
        in_specs=[
            pl.BlockSpec((tm, d), lambda i: (i, 0)),
            _resident((1, d)),
            _resident((d, 2 * d_ff)),
            _resident((d_ff, d)),
            _resident((1, d)),
        ],
        out_specs=pl.BlockSpec((tm, d), lambda i: (i, 0)),
        out_shape=jax.ShapeDtypeStruct((rows, d), F32),
        compiler_params=pltpu.CompilerParams(
            dimension_semantics=("arbitrary",),
            vmem_limit_bytes=V7X_VMEM_BYTES * 3 // 4),
        name="ffn_final" if final else "ffn",
    )(x2, norm_w, w_in, w_out, final_w)


def _causal_masks():
    row = lax.broadcasted_iota(jnp.int32, (SCAN_CHUNK, SCAN_CHUNK), 0)
    col = lax.broadcasted_iota(jnp.int32, (SCAN_CHUNK, SCAN_CHUNK), 1)
    return row, col


def _unit_lower_inverse(m, eye):
    s = eye + m
    mk = _dot_split(m, m)
    k = 2
    while 2 * k < SCAN_CHUNK:
        r = _dot_split(jnp.concatenate([s, mk], axis=0), mk)
        s = s + r[:SCAN_CHUNK]
        mk = r[SCAN_CHUNK:]
        k *= 2
    return s + _dot_split(s, mk)


def _mixer_body(h_ref, cos_ref, sin_ref, mixw_ref, wqkv_ref, wz_ref, wba_ref, wrq_ref,
                wrk_ref, wrv_ref, wrg_ref, wga_ref, wgb_ref, cw_ref, alog_ref, dtb_ref,
                gn_ref, rn_ref, wbg_ref, wbr_ref, wout_ref, s0_ref, r0_ref, tail0_ref,
                *rest, meta):
    if meta:
        s_out, r_out, tail_out = rest[:3]
        scratch = rest[3:]
    else:
        (out_ref,) = rest[:1]
        scratch = rest[1:]
    (cbuf, pz, prq, prk, prv, prg, ya, yb, s_ref, r_ref, dr_ref, xi_ref, zeta_ref) = scratch

    tt = h_ref.shape[0]
    n_chunks = tt // SCAN_CHUNK
    first_tile = pl.program_id(1) == 0
    row, col = _causal_masks()
    causal = row >= col
    strict = row > col
    eye = (row == col).astype(F32)
    ltri = causal.astype(BF16)

    @pl.when(jnp.logical_and(pl.program_id(0) == 0, first_tile))
    def _init_tables():
        rowf = row.astype(F32)
        dist = jnp.where(causal, (row - col).astype(F32), 0.0)
        for h in range(HEADS):
            lg = RET_LOG_GAMMA[h]
            dr_ref[h] = jnp.where(causal, jnp.exp(dist * lg), 0.0)
            xi_ref[h] = jnp.exp((rowf + 1.0) * lg)
            zeta_ref[h] = jnp.exp((SCAN_CHUNK - 1.0 - rowf) * lg)

    @pl.when(first_tile)
    def _init_state():
        s_ref[...] = s0_ref[...]
        r_ref[...] = r0_ref[...]
        cbuf[0:SUBLANES, :] = tail0_ref[...]

    x = h_ref[...]
    n = _rms(x, mixw_ref[...]).astype(BF16)
    proj = lambda w_ref: jnp.dot(n, w_ref[...], preferred_element_type=F32)
    cbuf[SUBLANES:SUBLANES + tt, :] = proj(wqkv_ref)
    ba = proj(wba_ref)
    prq[...] = proj(wrq_ref)
    prk[...] = proj(wrk_ref)
    prv[...] = proj(wrv_ref)
    if not meta:
        pz[...] = proj(wz_ref)
        prg[...] = proj(wrg_ref)

    beta_all = _sigmoid(ba)
    g_all = -jnp.exp(alog_ref[...]) * _softplus(ba + dtb_ref[...])

    def conv_silu(r0, c0):
        cs = slice(c0, c0 + HEAD_DIM)
        y = None
        for tap in range(CONV_K):
            lo = SUBLANES + r0 - (CONV_K - 1 - tap)
            term = cw_ref[tap:tap + 1, cs] * cbuf[lo:lo + SCAN_CHUNK, cs]
            y = term if y is None else y + term
        return y * _sigmoid(y)

    def l2norm(t):
        return t * lax.rsqrt(jnp.sum(t * t, axis=-1, keepdims=True) + EPS)

    def lane_bcast(a, lane):
        return jnp.broadcast_to(a[:, lane:lane + 1], (SCAN_CHUNK, LANES))

    def rotary(t, cos, sin):
        return t * cos + pltpu.roll(t, LANES // 2, axis=1) * sin

    qk_scale = HEAD_DIM ** -0.5
    for c in range(n_chunks):
        r0 = c * SCAN_CHUNK
        rows = slice(r0, r0 + SCAN_CHUNK)
        g_c = g_all[rows]
        g_hi = g_c.astype(BF16)
        g_r1 = g_c - g_hi.astype(F32)
        g_mid = g_r1.astype(BF16)
        g_lo = (g_r1 - g_mid.astype(F32)).astype(BF16)
        cs_dot = functools.partial(jnp.dot, preferred_element_type=F32)
        gc = cs_dot(ltri, g_hi) + (cs_dot(ltri, g_mid) + cs_dot(ltri, g_lo))
        gc_t = gc.T
        beta_c = beta_all[rows]
        cos = cos_ref[rows, :]
        sin = sin_ref[rows, :]

        for h in range(HEADS):
            hs = slice(h * HEAD_DIM, (h + 1) * HEAD_DIM)
            q = l2norm(conv_silu(r0, h * HEAD_DIM)) * qk_scale
            k = l2norm(conv_silu(r0, (HEADS + h) * HEAD_DIM))
            v = conv_silu(r0, (2 * HEADS + h) * HEAD_DIM)
            gcol = lane_bcast(gc, HEADS + h)
            grow = gc_t[HEADS + h:HEADS + h + 1, :]
            bcol = lane_bcast(beta_c, h)
            decay = jnp.where(causal, jnp.exp(jnp.where(causal, gcol - grow, 0.0)), 0.0)
            kq = _dot_nt(jnp.concatenate([k, q], axis=0), k)
            m = jnp.where(strict, -(bcol * kq[:SCAN_CHUNK] * decay), 0.0)
            qk = kq[SCAN_CHUNK:] * decay
            t_inv = _unit_lower_inverse(m, eye)
            eg = jnp.exp(gcol)
            uw = _dot(t_inv, jnp.concatenate([bcol * v, bcol * eg * k], axis=1))
            u = uw[:, :HEAD_DIM]
            w = uw[:, HEAD_DIM:]
            g_last = gcol[SCAN_CHUNK - 1:SCAN_CHUNK, :]
            k_dec = k * jnp.exp(g_last - gcol)
            state = s_ref[h]
            wq_s = _dot(jnp.concatenate([w, q * eg], axis=0), state)
            v_new = u - wq_s[:SCAN_CHUNK]
            s_ref[h] = state * jnp.exp(g_last) + _dot_tn(k_dec, v_new)
            if not meta:
                o = wq_s[SCAN_CHUNK:] + _dot(qk, v_new)
                o = o * lax.rsqrt(jnp.mean(o * o, axis=-1, keepdims=True) + EPS) * gn_ref[...]
                z = pz[rows, hs]
                ya[rows, hs] = (o * (z * _sigmoid(z))).astype(BF16)

            rq = rotary(prq[rows, hs], cos, sin)
            rk = rotary(prk[rows, hs], cos, sin) * qk_scale
            rv = prv[rows, hs]
            rstate = r_ref[h]
            r_ref[h] = (rstate * math.exp(SCAN_CHUNK * RET_LOG_GAMMA[h])
                        + _dot_tn(rk * zeta_ref[h], rv))
            if not meta:
                scores = _dot_nt(rq, rk) * dr_ref[h]
                ob = _dot(jnp.concatenate([scores, rq * xi_ref[h]], axis=1),
                          jnp.concatenate([rv, rstate], axis=0))
                mu = jnp.mean(ob, axis=-1, keepdims=True)
                cen = ob - mu
                var = jnp.mean(cen * cen, axis=-1, keepdims=True)
                ob = cen * lax.rsqrt(var + EPS) * rn_ref[:, hs]
                rg = prg[rows, hs]
                yb[rows, hs] = (rg * _sigmoid(rg) * ob).astype(BF16)

    cbuf[0:SUBLANES, :] = cbuf[tt:tt + SUBLANES, :]

    if meta:
        s_out[...] = s_ref[...]
        r_out[...] = r_ref[...]
        tail_out[...] = cbuf[0:SUBLANES, :]
    else:
        merged = (_sigmoid(proj(wga_ref)) * jnp.dot(ya[...], wbg_ref[...], preferred_element_type=F32)
                  + _sigmoid(proj(wgb_ref)) * jnp.dot(yb[...], wbr_ref[...], preferred_element_type=F32))
        out_ref[...] = x + jnp.dot(merged.astype(BF16), wout_ref[...], preferred_element_type=F32)


def _mixer(h2, cos, sin, params, s0, r0, tail0, *, batch, meta):
    rows, d = h2.shape
    seq = rows // batch
    tt = SCAN_CHUNK if meta else MIXER_ROWS
    assert seq % tt == 0
    nt = seq // tt
    qkv_cols = 3 * HEADS * HEAD_DIM
    state_shape = (HEADS, HEAD_DIM, HEAD_DIM)
    in_specs = [
        pl.BlockSpec((tt, d), lambda b, i: (b * nt + i, 0)),
        pl.BlockSpec((tt, LANES), lambda b, i: (i, 0)),
        pl.BlockSpec((tt, LANES), lambda b, i: (i, 0)),
    ] + [_resident(p.shape) for p in params] + [
        _resident(state_shape), _resident(state_shape), _resident((SUBLANES, qkv_cols))]
    if meta:
        out_shape = [jax.ShapeDtypeStruct(state_shape, F32), jax.ShapeDtypeStruct(state_shape, F32),
                     jax.ShapeDtypeStruct((SUBLANES, qkv_cols), F32)]
        out_specs = [pl.BlockSpec(state_shape, lambda b, i: (0, 0, 0)),
                     pl.BlockSpec(state_shape, lambda b, i: (0, 0, 0)),
                     pl.BlockSpec((SUBLANES, qkv_cols), lambda b, i: (0, 0))]
    else:
        out_shape = jax.ShapeDtypeStruct((rows, d), F32)
        out_specs = pl.BlockSpec((tt, d), lambda b, i: (b * nt + i, 0))
    act = lambda dt: pltpu.VMEM((tt, d), dt)
    table = pltpu.VMEM(state_shape, F32)
    scratch = [pltpu.VMEM((SUBLANES + tt, qkv_cols), F32),
               act(F32), act(F32), act(F32), act(F32), act(F32), act(BF16), act(BF16),
               table, table, table, table, table]
    return pl.pallas_call(
        functools.partial(_mixer_body, meta=meta),
        grid=(batch, nt),
        in_specs=in_specs,
        out_specs=out_specs,
        out_shape=out_shape,
        scratch_shapes=scratch,
        compiler_params=pltpu.CompilerParams(
            dimension_semantics=("arbitrary", "arbitrary"),
            vmem_limit_bytes=V7X_VMEM_BYTES * 7 // 8),
        name="mixer_meta" if meta else "mixer",
    )(h2, cos, sin, *params, s0, r0, tail0)


def _deinterleave_heads(w):
    d_in = w.shape[0]
    perm = jnp.concatenate([jnp.arange(0, HEAD_DIM, 2), jnp.arange(1, HEAD_DIM, 2)])
    return w.reshape(d_in, HEADS, HEAD_DIM)[:, :, perm].reshape(d_in, HEADS * HEAD_DIM)


def kernel(x, meta_tokens, ffn1_norm, ffn1_w_in, ffn1_w_out, mix_norm, w_in, gdn_conv_w, gdn_a_log, gdn_dt_bias, gdn_out_norm, ret_out_norm, w_branch_gdn, w_branch_ret, w_out, ffn2_norm, ffn2_w_in, ffn2_w_out, final_norm):
    batch, seq, d = x.shape
    assert ffn1_norm.shape[0] == 1, "single-layer stack only"
    assert meta_tokens.shape[0] == N_META and gdn_conv_w.shape[1] == CONV_K
    hd = HEADS * HEAD_DIM
    row = lambda v: v.reshape(1, -1).astype(F32)

    w = w_in[0]
    bounds = [0, 3 * hd, 4 * hd, 4 * hd + HEADS, 4 * hd + 2 * HEADS]
    for _ in range(6):
        bounds.append(bounds[-1] + hd)
    assert bounds[-1] == w.shape[1]
    piece = lambda j: w[:, bounds[j]:bounds[j + 1]]
    w_ba = jnp.zeros((d, LANES), F32).at[:, :2 * HEADS].set(jnp.concatenate([piece(2), piece(3)], axis=1))
    gate_vec = lambda v: jnp.zeros((1, LANES), F32).at[0, HEADS:2 * HEADS].set(v[0].astype(F32))
    conv_w = jnp.zeros((SUBLANES, 3 * hd), F32).at[:CONV_K].set(gdn_conv_w[0].astype(F32))
    bf = lambda a: a.astype(BF16)
    params = (
        row(mix_norm[0]), bf(piece(0)), bf(piece(1)), bf(w_ba),
        bf(_deinterleave_heads(piece(4))), bf(_deinterleave_heads(piece(5))), bf(piece(6)), bf(piece(7)),
        bf(piece(8)), bf(piece(9)), conv_w, gate_vec(gdn_a_log), gate_vec(gdn_dt_bias),
        row(gdn_out_norm[0]), row(ret_out_norm[0]),
        bf(w_branch_gdn[0]), bf(w_branch_ret[0]), bf(w_out[0]),
    )
    ffn1 = (row(ffn1_norm[0]), bf(ffn1_w_in[0]), bf(ffn1_w_out[0]), row(final_norm))
    ffn2 = (row(ffn2_norm[0]), bf(ffn2_w_in[0]), bf(ffn2_w_out[0]), row(final_norm))

    inv = 1.0 / (ROPE_BASE ** jnp.linspace(0.0, 1.0, HEAD_DIM // 2, dtype=F32))
    inv2 = jnp.concatenate([inv, inv]).reshape(1, LANES)

    pad = SCAN_CHUNK - N_META
    meta_block = jnp.concatenate([jnp.zeros((pad, d), F32), meta_tokens.astype(F32)], axis=0)
    cos_m, sin_m = _rope_tables(inv2, SCAN_CHUNK, -pad)
    zero_state = jnp.zeros((HEADS, HEAD_DIM, HEAD_DIM), F32)
    h_meta = _ffn(meta_block, *ffn1, final=False)
    s_m, r_m, tail_m = _mixer(h_meta, cos_m, sin_m, params, zero_state, zero_state,
                              jnp.zeros((SUBLANES, 3 * hd), F32), batch=1, meta=True)

    cos_r, sin_r = _rope_tables(inv2, seq, N_META)
    h = _ffn(x.reshape(batch * seq, d), *ffn1, final=False)
    h = _mixer(h, cos_r, sin_r, params, s_m, r_m, tail_m, batch=batch, meta=False)
    h = _ffn(h, *ffn2, final=True)
    return h.reshape(batch, seq, d)
```

```python
import functools
import math

import jax
import jax.numpy as jnp
from jax import lax
from jax.experimental import pallas as pl
from jax.experimental.pallas import tpu as pltpu

F32 = jnp.float32
BF16 = jnp.bfloat16

EPS = 1e-6
N_META = 16
HEADS = 8
HEAD_DIM = 128
CONV_K = 4
ROPE_BASE = 10000.0
RET_LOG_GAMMA = tuple(math.log1p(-(2.0 ** (-5.0 - h))) for h in range(HEADS))

LANES = 128
SUBLANES = 8
V7X_VMEM_BYTES = 64 * 1024 * 1024

SCAN_CHUNK = LANES
MIXER_ROWS = 2 * SCAN_CHUNK
FFN_ROWS = 512
FFN_COLS = 256
ROPE_ROWS = 256


def _dot(a, b):
    return jnp.dot(a.astype(BF16), b.astype(BF16), preferred_element_type=F32)


def _dot_nt(a, b):
    return lax.dot_general(a.astype(BF16), b.astype(BF16), (((1,), (1,)), ((), ())),
                           preferred_element_type=F32)


def _dot_tn(a, b):
    return lax.dot_general(a.astype(BF16), b.astype(BF16), (((0,), (0,)), ((), ())),
                           preferred_element_type=F32)


def _split2(a):
    hi = a.astype(BF16)
    lo = (a - hi.astype(F32)).astype(BF16)
    return hi, lo


def _dot_split(a, b):
    ah, al = _split2(a)
    bh, bl = _split2(b)
    d = functools.partial(jnp.dot, preferred_element_type=F32)
    return d(ah, bh) + (d(ah, bl) + d(al, bh))


def _sigmoid(x):
    return 1.0 / (1.0 + jnp.exp(-x))


def _softplus(x):
    return jnp.maximum(x, 0.0) + jnp.log1p(jnp.exp(-jnp.abs(x)))


def _rms(x, w):
    return x * lax.rsqrt(jnp.mean(x * x, axis=-1, keepdims=True) + EPS) * w


def _rope_body(inv_ref, cos_ref, sin_ref, *, pos0):
    rows = cos_ref.shape[0]
    row = lax.broadcasted_iota(jnp.int32, (rows, LANES), 0)
    lane = lax.broadcasted_iota(jnp.int32, (rows, LANES), 1)
    pos = (row + (pl.program_id(0) * rows + pos0)).astype(F32)
    ang = pos * inv_ref[...]
    cos_ref[...] = jnp.cos(ang)
    sin_ref[...] = jnp.where(lane < LANES // 2, -1.0, 1.0) * jnp.sin(ang)


def _rope_tables(inv2, n_rows, pos0):
    rows = min(ROPE_ROWS, n_rows)
    assert n_rows % rows == 0
    out = jax.ShapeDtypeStruct((n_rows, LANES), F32)
    spec = pl.BlockSpec((rows, LANES), lambda i: (i, 0))
    return pl.pallas_call(
        functools.partial(_rope_body, pos0=pos0),
        grid=(n_rows // rows,),
        in_specs=[pl.BlockSpec((1, LANES), lambda i: (0, 0))],
        out_specs=[spec, spec],
        out_shape=[out, out],
        name="rope_tables",
    )(inv2)


def _ffn_body(x_ref, nw_ref, win_ref, wout_ref, fw_ref, o_ref, *, final):
    d_ff = wout_ref.shape[0]
    x = x_ref[...]
    n = _rms(x, nw_ref[...]).astype(BF16)
    acc = jnp.zeros(x.shape, F32)
    for f in range(0, d_ff, FFN_COLS):
        gate = jnp.dot(n, win_ref[:, f:f + FFN_COLS], preferred_element_type=F32)
        up = jnp.dot(n, win_ref[:, d_ff + f:d_ff + f + FFN_COLS], preferred_element_type=F32)
        act = (gate * _sigmoid(gate) * up).astype(BF16)
        acc = acc + jnp.dot(act, wout_ref[f:f + FFN_COLS, :], preferred_element_type=F32)
    h = x + 0.5 * acc
    if final:
        h = _rms(h, fw_ref[...])
    o_ref[...] = h


def _resident(shape):
    return pl.BlockSpec(shape, lambda *_: (0,) * len(shape), pipeline_mode=pl.Buffered(1))


def _ffn(x2, norm_w, w_in, w_out, final_w, *, final):
    rows, d = x2.shape
    d_ff = w_out.shape[0]
    assert d_ff % FFN_COLS == 0
    tm = min(FFN_ROWS, rows)
    assert rows % tm == 0
    return pl.pallas_call(
        functools.partial(_ffn_body, final=final),
        grid=(rows // tm,),
        in_specs=[
            pl.BlockSpec((tm, d), lambda i: (i, 0)),
            _resident((1, d)),
            _resident((d, 2 * d_ff)),
            _resident((d_ff, d)),
            _resident((1, d)),
        ],
        out_specs=pl.BlockSpec((tm, d), lambda i: (i, 0)),
        out_shape=jax.ShapeDtypeStruct((rows, d), F32),
        compiler_params=pltpu.CompilerParams(
            dimension_semantics=("arbitrary",),
            vmem_limit_bytes=V7X_VMEM_BYTES * 3 // 4),
        name="ffn_final" if final else "ffn",
    )(x2, norm_w, w_in, w_out, final_w)


def _causal_masks():
    row = lax.broadcasted_iota(jnp.int32, (SCAN_CHUNK, SCAN_CHUNK), 0)
    col = lax.broadcasted_iota(jnp.int32, (SCAN_CHUNK, SCAN_CHUNK), 1)
    return row, col


def _unit_lower_inverse(ms, eye):
    ss = [eye + m for m in ms]
    mks = [_dot_split(m, m) for m in ms]
    k = 2
    while 2 * k < SCAN_CHUNK:
        rs = [_dot_split(jnp.concatenate([s, mk], axis=0), mk) for s, mk in zip(ss, mks)]
        ss = [s + r[:SCAN_CHUNK] for s, r in zip(ss, rs)]
        mks = [r[SCAN_CHUNK:] for r in rs]
        k *= 2
    return [s + _dot_split(s, mk) for s, mk in zip(ss, mks)]


def _mixer_body(h_ref, cos_ref, sin_ref, mixw_ref, wqkv_ref, wz_ref, wba_ref, wrq_ref,
                wrk_ref, wrv_ref, wrg_ref, wga_ref, wgb_ref, cw_ref, alog_ref, dtb_ref,
                gn_ref, rn_ref, wbg_ref, wbr_ref, wout_ref, s0_ref, r0_ref, tail0_ref,
                *rest, meta):
    if meta:
        s_out, r_out, tail_out = rest[:3]
        scratch = rest[3:]
    else:
        (out_ref,) = rest[:1]
        scratch = rest[1:]
    (cbuf, pz, prq, prk, prv, prg, ya, yb, s_ref, r_ref, dr_ref, xi_ref, zeta_ref) = scratch

    tt = h_ref.shape[0]
    n_chunks = tt // SCAN_CHUNK
    first_tile = pl.program_id(1) == 0
    row, col = _causal_masks()
    causal = row >= col
    strict = row > col
    eye = (row == col).astype(F32)
    ltri = causal.astype(BF16)

    @pl.when(jnp.logical_and(pl.program_id(0) == 0, first_tile))
    def _init_tables():
        rowf = row.astype(F32)
        dist = jnp.where(causal, (row - col).astype(F32), 0.0)
        for h in range(HEADS):
            lg = RET_LOG_GAMMA[h]
            dr_ref[h] = jnp.where(causal, jnp.exp(dist * lg), 0.0)
            xi_ref[h] = jnp.exp((rowf + 1.0) * lg)
            zeta_ref[h] = jnp.exp((SCAN_CHUNK - 1.0 - rowf) * lg)

    @pl.when(first_tile)
    def _init_state():
        s_ref[...] = s0_ref[...]
        r_ref[...] = r0_ref[...]
        cbuf[0:SUBLANES, :] = tail0_ref[...]

    x = h_ref[...]
    n = _rms(x, mixw_ref[...]).astype(BF16)
    proj = lambda w_ref: jnp.dot(n, w_ref[...], preferred_element_type=F32)
    cbuf[SUBLANES:SUBLANES + tt, :] = proj(wqkv_ref)
    ba = proj(wba_ref)
    prq[...] = proj(wrq_ref)
    prk[...] = proj(wrk_ref)
    prv[...] = proj(wrv_ref)
    if not meta:
        pz[...] = proj(wz_ref)
        prg[...] = proj(wrg_ref)

    beta_all = _sigmoid(ba)
    g_all = -jnp.exp(alog_ref[...]) * _softplus(ba + dtb_ref[...])

    def conv_silu(r0, c0):
        cs = slice(c0, c0 + HEAD_DIM)
        y = None
        for tap in range(CONV_K):
            lo = SUBLANES + r0 - (CONV_K - 1 - tap)
            term = cw_ref[tap:tap + 1, cs] * cbuf[lo:lo + SCAN_CHUNK, cs]
            y = term if y is None else y + term
        return y * _sigmoid(y)

    def l2norm(t):
        return t * lax.rsqrt(jnp.sum(t * t, axis=-1, keepdims=True) + EPS)

    def lane_bcast(a, lane):
        return jnp.broadcast_to(a[:, lane:lane + 1], (SCAN_CHUNK, LANES))

    def rotary(t, cos, sin):
        return t * cos + pltpu.roll(t, LANES // 2, axis=1) * sin

    qk_scale = HEAD_DIM ** -0.5
    for c in range(n_chunks):
        r0 = c * SCAN_CHUNK
        rows = slice(r0, r0 + SCAN_CHUNK)
        g_c = g_all[rows]
        g_hi = g_c.astype(BF16)
        g_r1 = g_c - g_hi.astype(F32)
        g_mid = g_r1.astype(BF16)
        g_lo = (g_r1 - g_mid.astype(F32)).astype(BF16)
        cs_dot = functools.partial(jnp.dot, preferred_element_type=F32)
        gc = cs_dot(ltri, g_hi) + (cs_dot(ltri, g_mid) + cs_dot(ltri, g_lo))
        gc_t = gc.T
        beta_c = beta_all[rows]
        cos = cos_ref[rows, :]
        sin = sin_ref[rows, :]

        heads = range(HEADS)
        hsl = [slice(h * HEAD_DIM, (h + 1) * HEAD_DIM) for h in heads]

        q = [l2norm(conv_silu(r0, h * HEAD_DIM)) * qk_scale for h in heads]
        k = [l2norm(conv_silu(r0, (HEADS + h) * HEAD_DIM)) for h in heads]
        v = [conv_silu(r0, (2 * HEADS + h) * HEAD_DIM) for h in heads]
        gcol = [lane_bcast(gc, HEADS + h) for h in heads]
        grow = [gc_t[HEADS + h:HEADS + h + 1, :] for h in heads]
        bcol = [lane_bcast(beta_c, h) for h in heads]
        decay = [jnp.where(causal, jnp.exp(jnp.where(causal, gcol[h] - grow[h], 0.0)), 0.0)
                 for h in heads]
        kq = [_dot_nt(jnp.concatenate([k[h], q[h]], axis=0), k[h]) for h in heads]
        rq = [rotary(prq[rows, hsl[h]], cos, sin) for h in heads]
        rk = [rotary(prk[rows, hsl[h]], cos, sin) * qk_scale for h in heads]
        rv = [prv[rows, hsl[h]] for h in heads]
        if not meta:
            scores = [_dot_nt(rq[h], rk[h]) * dr_ref[h] for h in heads]
        m = [jnp.where(strict, -(bcol[h] * kq[h][:SCAN_CHUNK] * decay[h]), 0.0) for h in heads]
        qk = [kq[h][SCAN_CHUNK:] * decay[h] for h in heads]
        t_inv = _unit_lower_inverse(m, eye)
        eg = [jnp.exp(gcol[h]) for h in heads]
        uw = [_dot(t_inv[h], jnp.concatenate([bcol[h] * v[h], bcol[h] * eg[h] * k[h]], axis=1))
              for h in heads]
        g_last = [gcol[h][SCAN_CHUNK - 1:SCAN_CHUNK, :] for h in heads]
        k_dec = [k[h] * jnp.exp(g_last[h] - gcol[h]) for h in heads]

        state = [s_ref[h] for h in heads]
        rstate = [r_ref[h] for h in heads]
        wq_s = [_dot(jnp.concatenate([uw[h][:, HEAD_DIM:], q[h] * eg[h]], axis=0), state[h])
                for h in heads]
        rdelta = [_dot_tn(rk[h] * zeta_ref[h], rv[h]) for h in heads]
        v_new = [uw[h][:, :HEAD_DIM] - wq_s[h][:SCAN_CHUNK] for h in heads]
        sdelta = [_dot_tn(k_dec[h], v_new[h]) for h in heads]
        for h in heads:
            s_ref[h] = state[h] * jnp.exp(g_last[h]) + sdelta[h]
            r_ref[h] = rstate[h] * math.exp(SCAN_CHUNK * RET_LOG_GAMMA[h]) + rdelta[h]
        if not meta:
            o = [wq_s[h][SCAN_CHUNK:] + _dot(qk[h], v_new[h]) for h in heads]
            ob = [_dot(jnp.concatenate([scores[h], rq[h] * xi_ref[h]], axis=1),
                       jnp.concatenate([rv[h], rstate[h]], axis=0)) for h in heads]
            for h in heads:
                oh = o[h] * lax.rsqrt(jnp.mean(o[h] * o[h], axis=-1, keepdims=True) + EPS) * gn_ref[...]
                z = pz[rows, hsl[h]]
                ya[rows, hsl[h]] = (oh * (z * _sigmoid(z))).astype(BF16)
                mu = jnp.mean(ob[h], axis=-1, keepdims=True)
                cen = ob[h] - mu
                var = jnp.mean(cen * cen, axis=-1, keepdims=True)
                obn = cen * lax.rsqrt(var + EPS) * rn_ref[:, hsl[h]]
                rg = prg[rows, hsl[h]]
                yb[rows, hsl[h]] = (rg * _sigmoid(rg) * obn).astype(BF16)

    cbuf[0:SUBLANES, :] = cbuf[tt:tt + SUBLANES, :]

    if meta:
        s_out[...] = s_ref[...]
        r_out[...] = r_ref[...]
        tail_out[...] = cbuf[0:SUBLANES, :]
    else:
        merged = (_sigmoid(proj(wga_ref)) * jnp.dot(ya[...], wbg_ref[...], preferred_element_type=F32)
                  + _sigmoid(proj(wgb_ref)) * jnp.dot(yb[...], wbr_ref[...], preferred_element_type=F32))
        out_ref[...] = x + jnp.dot(merged.astype(BF16), wout_ref[...], preferred_element_type=F32)


def _mixer(h2, cos, sin, params, s0, r0, tail0, *, batch, meta):
    rows, d = h2.shape
    seq = rows // batch
    tt = SCAN_CHUNK if meta else MIXER_ROWS
    assert seq % tt == 0
    nt = seq // tt
    qkv_cols = 3 * HEADS * HEAD_DIM
    state_shape = (HEADS, HEAD_DIM, HEAD_DIM)
    in_specs = [
        pl.BlockSpec((tt, d), lambda b, i: (b * nt + i, 0)),
        pl.BlockSpec((tt, LANES), lambda b, i: (i, 0)),
        pl.BlockSpec((tt, LANES), lambda b, i: (i, 0)),
    ] + [_resident(p.shape) for p in params] + [
        _resident(state_shape), _resident(state_shape), _resident((SUBLANES, qkv_cols))]
    if meta:
        out_shape = [jax.ShapeDtypeStruct(state_shape, F32), jax.ShapeDtypeStruct(state_shape, F32),
                     jax.ShapeDtypeStruct((SUBLANES, qkv_cols), F32)]
        out_specs = [pl.BlockSpec(state_shape, lambda b, i: (0, 0, 0)),
                     pl.BlockSpec(state_shape, lambda b, i: (0, 0, 0)),
                     pl.BlockSpec((SUBLANES, qkv_cols), lambda b, i: (0, 0))]
    else:
        out_shape = jax.ShapeDtypeStruct((rows, d), F32)
        out_specs = pl.BlockSpec((tt, d), lambda b, i: (b * nt + i, 0))
    act = lambda dt: pltpu.VMEM((tt, d), dt)
    table = pltpu.VMEM(state_shape, F32)
    scratch = [pltpu.VMEM((SUBLANES + tt, qkv_cols), F32),
               act(F32), act(F32), act(F32), act(F32), act(F32), act(BF16), act(BF16),
               table, table, table, table, table]
    return pl.pallas_call(
        functools.partial(_mixer_body, meta=meta),
        grid=(batch, nt),
        in_specs=in_specs,
        out_specs=out_specs,
        out_shape=out_shape,
        scratch_shapes=scratch,
        compiler_params=pltpu.CompilerParams(
            dimension_semantics=("arbitrary", "arbitrary"),
            vmem_limit_bytes=V7X_VMEM_BYTES * 7 // 8),
        name="mixer_meta" if meta else "mixer",
    )(h2, cos, sin, *params, s0, r0, tail0)


def _deinterleave_heads(w):
    d_in = w.shape[0]
    perm = jnp.concatenate([jnp.arange(0, HEAD_DIM, 2), jnp.arange(1, HEAD_DIM, 2)])
    return w.reshape(d_in, HEADS, HEAD_DIM)[:, :, perm].reshape(d_in, HEADS * HEAD_DIM)


def kernel(x, meta_tokens, ffn1_norm, ffn1_w_in, ffn1_w_out, mix_norm, w_in, gdn_conv_w, gdn_a_log, gdn_dt_bias, gdn_out_norm, ret_out_norm, w_branch_gdn, w_branch_ret, w_out, ffn2_norm, ffn2_w_in, ffn2_w_out, final_norm):
    batch, seq, d = x.shape
    assert ffn1_norm.shape[0] == 1, "single-layer stack only"
    assert meta_tokens.shape[0] == N_META and gdn_conv_w.shape[1] == CONV_K
    hd = HEADS * HEAD_DIM
    row = lambda v: v.reshape(1, -1).astype(F32)

    w = w_in[0]
    bounds = [0, 3 * hd, 4 * hd, 4 * hd + HEADS, 4 * hd + 2 * HEADS]
    for _ in range(6):
        bounds.append(bounds[-1] + hd)
    assert bounds[-1] == w.shape[1]
    piece = lambda j: w[:, bounds[j]:bounds[j + 1]]
    w_ba = jnp.zeros((d, LANES), F32).at[:, :2 * HEADS].set(jnp.concatenate([piece(2), piece(3)], axis=1))
    gate_vec = lambda v: jnp.zeros((1, LANES), F32).at[0, HEADS:2 * HEADS].set(v[0].astype(F32))
    conv_w = jnp.zeros((SUBLANES, 3 * hd), F32).at[:CONV_K].set(gdn_conv_w[0].astype(F32))
    bf = lambda a: a.astype(BF16)
    params = (
        row(mix_norm[0]), bf(piece(0)), bf(piece(1)), bf(w_ba),
        bf(_deinterleave_heads(piece(4))), bf(_deinterleave_heads(piece(5))), bf(piece(6)), bf(piece(7)),
        bf(piece(8)), bf(piece(9)), conv_w, gate_vec(gdn_a_log), gate_vec(gdn_dt_bias),
        row(gdn_out_norm[0]), row(ret_out_norm[0]),
        bf(w_branch_gdn[0]), bf(w_branch_ret[0]), bf(w_out[0]),
    )
    ffn1 = (row(ffn1_norm[0]), bf(ffn1_w_in[0]), bf(ffn1_w_out[0]), row(final_norm))
    ffn2 = (row(ffn2_norm[0]), bf(ffn2_w_in[0]), bf(ffn2_w_out[0]), row(final_norm))

    inv = 1.0 / (ROPE_BASE ** jnp.linspace(0.0, 1.0, HEAD_DIM // 2, dtype=F32))
    inv2 = jnp.concatenate([inv, inv]).reshape(1, LANES)

    pad = SCAN_CHUNK - N_META
    meta_block = jnp.concatenate([jnp.zeros((pad, d), F32), meta_tokens.astype(F32)], axis=0)
    cos_m, sin_m = _rope_tables(inv2, SCAN_CHUNK, -pad)
    zero_state = jnp.zeros((HEADS, HEAD_DIM, HEAD_DIM), F32)
    h_meta = _ffn(meta_block, *ffn1, final=False)
    s_m, r_m, tail_m = _mixer(h_meta, cos_m, sin_m, params, zero_state, zero_state,
                              jnp.zeros((SUBLANES, 3 * hd), F32), batch=1, meta=True)

    cos_r, sin_r = _rope_tables(inv2, seq, N_META)
    h = _ffn(x.reshape(batch * seq, d), *ffn1, final=False)
    h = _mixer(h, cos_r, sin_r, params, s_m, r_m, tail_m, batch=batch, meta=False)
    h = _ffn(h, *ffn2, final=True)
    return h.reshape(batch, seq, d)
```

```python
import functools
import math

import jax
import jax.numpy as jnp
from jax import lax
from jax.experimental import pallas as pl
from jax.experimental.pallas import tpu as pltpu

F32 = jnp.float32
BF16 = jnp.bfloat16

EPS = 1e-6
N_META = 16
HEADS = 8
HEAD_DIM = 128
CONV_K = 4
ROPE_BASE = 10000.0
RET_LOG_GAMMA = tuple(math.log1p(-(2.0 ** (-5.0 - h))) for h in range(HEADS))

LANES = 128
SUBLANES = 8
V7X_VMEM_BYTES = 64 * 1024 * 1024

SCAN_CHUNK = LANES
SOLVE_BLOCK = 2 * SUBLANES
MIXER_ROWS = 2 * SCAN_CHUNK
FFN_ROWS = 512
FFN_COLS = 256
FILL_COLS = 256
ROPE_ROWS = 256


def _dot(a, b):
    return jnp.dot(a.astype(BF16), b.astype(BF16), preferred_element_type=F32)


def _dot_nt(a, b):
    return lax.dot_general(a.astype(BF16), b.astype(BF16), (((1,), (1,)), ((), ())),
                           preferred_element_type=F32)


def _dot_tn(a, b):
    return lax.dot_general(a.astype(BF16), b.astype(BF16), (((0,), (0,)), ((), ())),
                           preferred_element_type=F32)


def _sigmoid(x):
    return 1.0 / (1.0 + jnp.exp(-x))


def _softplus(x):
    return jnp.maximum(x, 0.0) + jnp.log1p(jnp.exp(-jnp.abs(x)))


def _rms(x, w):
    return x * lax.rsqrt(jnp.mean(x * x, axis=-1, keepdims=True) + EPS) * w


def _rope_body(inv_ref, cos_ref, sin_ref, *, pos0):
    rows = cos_ref.shape[0]
    row = lax.broadcasted_iota(jnp.int32, (rows, LANES), 0)
    lane = lax.broadcasted_iota(jnp.int32, (rows, LANES), 1)
    pos = (row + (pl.program_id(0) * rows + pos0)).astype(F32)
    ang = pos * inv_ref[...]
    cos_ref[...] = jnp.cos(ang)
    sin_ref[...] = jnp.where(lane < LANES // 2, -1.0, 1.0) * jnp.sin(ang)


def _rope_tables(inv2, n_rows, pos0):
    rows = min(ROPE_ROWS, n_rows)
    assert n_rows % rows == 0
    out = jax.ShapeDtypeStruct((n_rows, LANES), F32)
    spec = pl.BlockSpec((rows, LANES), lambda i: (i, 0))
    return pl.pallas_call(
        functools.partial(_rope_body, pos0=pos0),
        grid=(n_rows // rows,),
        in_specs=[pl.BlockSpec((1, LANES), lambda i: (0, 0))],
        out_specs=[spec, spec],
        out_shape=[out, out],
        name="rope_tables",
    )(inv2)


def _ffn_body(x_ref, nw_ref, win_ref, wout_ref, fw_ref, o_ref, *, final):
    d_ff = wout_ref.shape[0]
    x = x_ref[...]
    n = _rms(x, nw_ref[...]).astype(BF16)
    acc = jnp.zeros(x.shape, F32)
    for f in range(0, d_ff, FFN_COLS):
        gate = jnp.dot(n, win_ref[:, f:f + FFN_COLS], preferred_element_type=F32)
        up = jnp.dot(n, win_ref[:, d_ff + f:d_ff + f + FFN_COLS], preferred_element_type=F32)
        act = (gate * _sigmoid(gate) * up).astype(BF16)
        acc = acc + jnp.dot(act, wout_ref[f:f + FFN_COLS, :], preferred_element_type=F32)
    h = x + 0.5 * acc
    if final:
        h = _rms(h, fw_ref[...])
    o_ref[...] = h


def _resident(shape):
    return pl.BlockSpec(shape, lambda *_: (0,) * len(shape), pipeline_mode=pl.Buffered(1))


def _ffn(x2, norm_w, w_in, w_out, final_w, *, final):
    rows, d = x2.shape
    d_ff = w_out.shape[0]
    assert d_ff % FFN_COLS == 0
    tm = min(FFN_ROWS, rows)
    assert rows % tm == 0
    return pl.pallas_call(
        functools.partial(_ffn_body, final=final),
        grid=(rows // tm,),
        in_specs=[
            pl.BlockSpec((tm, d), lambda i: (i, 0)),
            _resident((1, d)),
            _resident((d, 2 * d_ff)),
            _resident((d_ff, d)),
            _resident((1, d)),
        ],
        out_specs=pl.BlockSpec((tm, d), lambda i: (i, 0)),
        out_shape=jax.ShapeDtypeStruct((rows, d), F32),
        compiler_params=pltpu.CompilerParams(
            dimension_semantics=("arbitrary",),
            vmem_limit_bytes=V7X_VMEM_BYTES * 3 // 4),
        name="ffn_final" if final else "ffn",
    )(x2, norm_w, w_in, w_out, final_w)


def _causal_masks():
    row = lax.broadcasted_iota(jnp.int32, (SCAN_CHUNK, SCAN_CHUNK), 0)
    col = lax.broadcasted_iota(jnp.int32, (SCAN_CHUNK, SCAN_CHUNK), 1)
    return row, col


def _unit_lower_solve(ns, rhs, row, col, fill):
    probs = range(len(ns))
    blk = (row // SOLVE_BLOCK) == (col // SOLVE_BLOCK)
    eye = (row == col).astype(F32)
    md = [jnp.where(blk, -n, 0.0) for n in ns]
    lo = [jnp.where(blk, 0.0, n) for n in ns]
    ss = [eye + m for m in md]
    mks = [_dot(m, m) for m in md]
    fill()
    k = 2
    while 2 * k < SOLVE_BLOCK:
        rs = [_dot(jnp.concatenate([s, mk], axis=0), mk) for s, mk in zip(ss, mks)]
        fill()
        ss = [s + r[:SCAN_CHUNK] for s, r in zip(ss, rs)]
        mks = [r[SCAN_CHUNK:] for r in rs]
        k *= 2
    td = [s + _dot(s, mk) for s, mk in zip(ss, mks)]
    fill()
    pb = [_dot(td[p], jnp.concatenate([lo[p], rhs[p]], axis=1)) for p in probs]
    fill()
    pl_b = [x[:, :SCAN_CHUNK].astype(BF16) for x in pb]
    bb = [x[:, SCAN_CHUNK:] for x in pb]
    xs = [[b[:SOLVE_BLOCK]] for b in bb]
    xs_b = [[x[0].astype(BF16)] for x in xs]
    for r0 in range(SOLVE_BLOCK, SCAN_CHUNK, SOLVE_BLOCK):
        rows = slice(r0, r0 + SOLVE_BLOCK)
        for p in probs:
            upd = jnp.dot(pl_b[p][rows, :r0], jnp.concatenate(xs_b[p], axis=0),
                          preferred_element_type=F32)
            xs[p].append(bb[p][rows] - upd)
            xs_b[p].append(xs[p][-1].astype(BF16))
        fill()
    return [jnp.concatenate(x, axis=0) for x in xs]


def _mixer_body(h_ref, cos_ref, sin_ref, mixw_ref, wqkv_ref, wz_ref, wba_ref, wrq_ref,
                wrk_ref, wrv_ref, wrg_ref, wga_ref, wgb_ref, cw_ref, alog_ref, dtb_ref,
                gn_ref, rn_ref, wbg_ref, wbr_ref, wout_ref, s0_ref, r0_ref, tail0_ref,
                *rest, meta):
    if meta:
        s_out, r_out, tail_out = rest[:3]
        scratch = rest[3:]
    else:
        (out_ref,) = rest[:1]
        scratch = rest[1:]
    (cbuf, nbuf, pz, prq, prk, prv, prg, pga, pgb, ya, yb,
     s_ref, r_ref, dr_ref, xi_ref, zeta_ref) = scratch

    tt = h_ref.shape[0]
    n_chunks = tt // SCAN_CHUNK
    first_tile = pl.program_id(1) == 0
    row, col = _causal_masks()
    causal = row >= col
    strict = row > col
    ltri = causal.astype(BF16)

    @pl.when(jnp.logical_and(pl.program_id(0) == 0, first_tile))
    def _init_tables():
        rowf = row.astype(F32)
        dist = jnp.where(causal, (row - col).astype(F32), 0.0)
        for h in range(HEADS):
            lg = RET_LOG_GAMMA[h]
            dr_ref[h] = jnp.where(causal, jnp.exp(dist * lg), 0.0)
            xi_ref[h] = jnp.exp((rowf + 1.0) * lg)
            zeta_ref[h] = jnp.exp((SCAN_CHUNK - 1.0 - rowf) * lg)

    @pl.when(first_tile)
    def _init_state():
        s_ref[...] = s0_ref[...]
        r_ref[...] = r0_ref[...]
        cbuf[0:SUBLANES, :] = tail0_ref[...]

    nbuf[...] = _rms(h_ref[...], mixw_ref[...]).astype(BF16)
    proj = lambda w_ref: jnp.dot(nbuf[...], w_ref[...], preferred_element_type=F32)
    cbuf[SUBLANES:SUBLANES + tt, :] = proj(wqkv_ref)
    ba = proj(wba_ref)
    prq[...] = proj(wrq_ref)
    prk[...] = proj(wrk_ref)
    prv[...] = proj(wrv_ref)

    def proj_piece(w_ref, dst, c0):
        cs = slice(c0, c0 + FILL_COLS)
        dst[:, cs] = jnp.dot(nbuf[...], w_ref[:, cs], preferred_element_type=F32)

    deferred = []
    if not meta:
        for w_ref, dst in ((wz_ref, pz), (wrg_ref, prg), (wga_ref, pga), (wgb_ref, pgb)):
            deferred += [functools.partial(proj_piece, w_ref, dst, c0)
                         for c0 in range(0, w_ref.shape[1], FILL_COLS)]
    n_out_gate_pieces = len(deferred) // 2

    def fill():
        if deferred:
            deferred.pop(0)()

    def fill_until(remaining):
        while len(deferred) > remaining:
            fill()

    beta_all = _sigmoid(ba)
    g_all = -jnp.exp(alog_ref[...]) * _softplus(ba + dtb_ref[...])

    def conv_silu(r0, c0):
        cs = slice(c0, c0 + HEAD_DIM)
        y = None
        for tap in range(CONV_K):
            lo = SUBLANES + r0 - (CONV_K - 1 - tap)
            term = cw_ref[tap:tap + 1, cs] * cbuf[lo:lo + SCAN_CHUNK, cs]
            y = term if y is None else y + term
        return y * _sigmoid(y)

    def l2norm(t):
        return t * lax.rsqrt(jnp.sum(t * t, axis=-1, keepdims=True) + EPS)

    def lane_bcast(a, lane):
        return jnp.broadcast_to(a[:, lane:lane + 1], (SCAN_CHUNK, LANES))

    def rotary(t, cos, sin):
        return t * cos + pltpu.roll(t, LANES // 2, axis=1) * sin

    qk_scale = HEAD_DIM ** -0.5
    for c in range(n_chunks):
        r0 = c * SCAN_CHUNK
        rows = slice(r0, r0 + SCAN_CHUNK)
        g_c = g_all[rows]
        g_hi = g_c.astype(BF16)
        g_r1 = g_c - g_hi.astype(F32)
        g_mid = g_r1.astype(BF16)
        g_lo = (g_r1 - g_mid.astype(F32)).astype(BF16)
        cs_dot = functools.partial(jnp.dot, preferred_element_type=F32)
        gc = cs_dot(ltri, g_hi) + (cs_dot(ltri, g_mid) + cs_dot(ltri, g_lo))
        gc_t = gc.T
        beta_c = beta_all[rows]
        cos = cos_ref[rows, :]
        sin = sin_ref[rows, :]

        heads = range(HEADS)
        hsl = [slice(h * HEAD_DIM, (h + 1) * HEAD_DIM) for h in heads]

        q = [l2norm(conv_silu(r0, h * HEAD_DIM)) * qk_scale for h in heads]
        k = [l2norm(conv_silu(r0, (HEADS + h) * HEAD_DIM)) for h in heads]
        v = [conv_silu(r0, (2 * HEADS + h) * HEAD_DIM) for h in heads]
        gcol = [lane_bcast(gc, HEADS + h) for h in heads]
        grow = [gc_t[HEADS + h:HEADS + h + 1, :] for h in heads]
        bcol = [lane_bcast(beta_c, h) for h in heads]
        decay = [jnp.where(causal, jnp.exp(jnp.where(causal, gcol[h] - grow[h], 0.0)), 0.0)
                 for h in heads]
        kq = [_dot_nt(jnp.concatenate([k[h], q[h]], axis=0), k[h]) for h in heads]
        rq = [rotary(prq[rows, hsl[h]], cos, sin) for h in heads]
        rk = [rotary(prk[rows, hsl[h]], cos, sin) * qk_scale for h in heads]
        rv = [prv[rows, hsl[h]] for h in heads]
        if not meta:
            scores = [_dot_nt(rq[h], rk[h]) * dr_ref[h] for h in heads]
        a_low = [jnp.where(strict, bcol[h] * kq[h][:SCAN_CHUNK] * decay[h], 0.0) for h in heads]
        qk = [kq[h][SCAN_CHUNK:] * decay[h] for h in heads]
        eg = [jnp.exp(gcol[h]) for h in heads]
        uw = _unit_lower_solve(
            a_low, [jnp.concatenate([bcol[h] * v[h], bcol[h] * eg[h] * k[h]], axis=1) for h in heads],
            row, col, fill)
        g_last = [gcol[h][SCAN_CHUNK - 1:SCAN_CHUNK, :] for h in heads]
        k_dec = [k[h] * jnp.exp(g_last[h] - gcol[h]) for h in heads]

        state = [s_ref[h] for h in heads]
        rstate = [r_ref[h] for h in heads]
        wq_s = [_dot(jnp.concatenate([uw[h][:, HEAD_DIM:], q[h] * eg[h]], axis=0), state[h])
                for h in heads]
        rdelta = [_dot_tn(rk[h] * zeta_ref[h], rv[h]) for h in heads]
        v_new = [uw[h][:, :HEAD_DIM] - wq_s[h][:SCAN_CHUNK] for h in heads]
        sdelta = [_dot_tn(k_dec[h], v_new[h]) for h in heads]
        for h in heads:
            s_ref[h] = state[h] * jnp.exp(g_last[h]) + sdelta[h]
            r_ref[h] = rstate[h] * math.exp(SCAN_CHUNK * RET_LOG_GAMMA[h]) + rdelta[h]
        if not meta:
            fill_until(n_out_gate_pieces)
            o = [wq_s[h][SCAN_CHUNK:] + _dot(qk[h], v_new[h]) for h in heads]
            ob = [_dot(jnp.concatenate([scores[h], rq[h] * xi_ref[h]], axis=1),
                       jnp.concatenate([rv[h], rstate[h]], axis=0)) for h in heads]
            for h in heads:
                oh = o[h] * lax.rsqrt(jnp.mean(o[h] * o[h], axis=-1, keepdims=True) + EPS) * gn_ref[...]
                z = pz[rows, hsl[h]]
                ya[rows, hsl[h]] = (oh * (z * _sigmoid(z))).astype(BF16)
                mu = jnp.mean(ob[h], axis=-1, keepdims=True)
                cen = ob[h] - mu
                var = jnp.mean(cen * cen, axis=-1, keepdims=True)
                obn = cen * lax.rsqrt(var + EPS) * rn_ref[:, hsl[h]]
                rg = prg[rows, hsl[h]]
                yb[rows, hsl[h]] = (rg * _sigmoid(rg) * obn).astype(BF16)

    cbuf[0:SUBLANES, :] = cbuf[tt:tt + SUBLANES, :]

    if meta:
        s_out[...] = s_ref[...]
        r_out[...] = r_ref[...]
        tail_out[...] = cbuf[0:SUBLANES, :]
    else:
        fill_until(0)
        merged = (_sigmoid(pga[...]) * jnp.dot(ya[...], wbg_ref[...], preferred_element_type=F32)
                  + _sigmoid(pgb[...]) * jnp.dot(yb[...], wbr_ref[...], preferred_element_type=F32))
        out_ref[...] = h_ref[...] + jnp.dot(merged.astype(BF16), wout_ref[...], preferred_element_type=F32)


def _mixer(h2, cos, sin, params, s0, r0, tail0, *, batch, meta):
    rows, d = h2.shape
    seq = rows // batch
    tt = SCAN_CHUNK if meta else MIXER_ROWS
    assert seq % tt == 0
    nt = seq // tt
    qkv_cols = 3 * HEADS * HEAD_DIM
    state_shape = (HEADS, HEAD_DIM, HEAD_DIM)
    in_specs = [
        pl.BlockSpec((tt, d), lambda b, i: (b * nt + i, 0)),
        pl.BlockSpec((tt, LANES), lambda b, i: (i, 0)),
        pl.BlockSpec((tt, LANES), lambda b, i: (i, 0)),
    ] + [_resident(p.shape) for p in params] + [
        _resident(state_shape), _resident(state_shape), _resident((SUBLANES, qkv_cols))]
    if meta:
        out_shape = [jax.ShapeDtypeStruct(state_shape, F32), jax.ShapeDtypeStruct(state_shape, F32),
                     jax.ShapeDtypeStruct((SUBLANES, qkv_cols), F32)]
        out_specs = [pl.BlockSpec(state_shape, lambda b, i: (0, 0, 0)),
                     pl.BlockSpec(state_shape, lambda b, i: (0, 0, 0)),
                     pl.BlockSpec((SUBLANES, qkv_cols), lambda b, i: (0, 0))]
    else:
        out_shape = jax.ShapeDtypeStruct((rows, d), F32)
        out_specs = pl.BlockSpec((tt, d), lambda b, i: (b * nt + i, 0))
    act = lambda dt: pltpu.VMEM((tt, d), dt)
    table = pltpu.VMEM(state_shape, F32)
    scratch = [pltpu.VMEM((SUBLANES + tt, qkv_cols), F32), act(BF16),
               act(F32), act(F32), act(F32), act(F32), act(F32), act(F32), act(F32),
               act(BF16), act(BF16), table, table, table, table, table]
    return pl.pallas_call(
        functools.partial(_mixer_body, meta=meta),
        grid=(batch, nt),
        in_specs=in_specs,
        out_specs=out_specs,
        out_shape=out_shape,
        scratch_shapes=scratch,
        compiler_params=pltpu.CompilerParams(
            dimension_semantics=("arbitrary", "arbitrary"),
            vmem_limit_bytes=V7X_VMEM_BYTES * 7 // 8),
        name="mixer_meta" if meta else "mixer",
    )(h2, cos, sin, *params, s0, r0, tail0)


def _deinterleave_heads(w):
    d_in = w.shape[0]
    perm = jnp.concatenate([jnp.arange(0, HEAD_DIM, 2), jnp.arange(1, HEAD_DIM, 2)])
    return w.reshape(d_in, HEADS, HEAD_DIM)[:, :, perm].reshape(d_in, HEADS * HEAD_DIM)


def kernel(x, meta_tokens, ffn1_norm, ffn1_w_in, ffn1_w_out, mix_norm, w_in, gdn_conv_w, gdn_a_log, gdn_dt_bias, gdn_out_norm, ret_out_norm, w_branch_gdn, w_branch_ret, w_out, ffn2_norm, ffn2_w_in, ffn2_w_out, final_norm):
    batch, seq, d = x.shape
    assert ffn1_norm.shape[0] == 1, "single-layer stack only"
    assert meta_tokens.shape[0] == N_META and gdn_conv_w.shape[1] == CONV_K
    hd = HEADS * HEAD_DIM
    row = lambda v: v.reshape(1, -1).astype(F32)

    w = w_in[0]
    bounds = [0, 3 * hd, 4 * hd, 4 * hd + HEADS, 4 * hd + 2 * HEADS]
    for _ in range(6):
        bounds.append(bounds[-1] + hd)
    assert bounds[-1] == w.shape[1]
    piece = lambda j: w[:, bounds[j]:bounds[j + 1]]
    w_ba = jnp.zeros((d, LANES), F32).at[:, :2 * HEADS].set(jnp.concatenate([piece(2), piece(3)], axis=1))
    gate_vec = lambda v: jnp.zeros((1, LANES), F32).at[0, HEADS:2 * HEADS].set(v[0].astype(F32))
    conv_w = jnp.zeros((SUBLANES, 3 * hd), F32).at[:CONV_K].set(gdn_conv_w[0].astype(F32))
    bf = lambda a: a.astype(BF16)
    params = (
        row(mix_norm[0]), bf(piece(0)), bf(piece(1)), bf(w_ba),
        bf(_deinterleave_heads(piece(4))), bf(_deinterleave_heads(piece(5))), bf(piece(6)), bf(piece(7)),
        bf(piece(8)), bf(piece(9)), conv_w, gate_vec(gdn_a_log), gate_vec(gdn_dt_bias),
        row(gdn_out_norm[0]), row(ret_out_norm[0]),
        bf(w_branch_gdn[0]), bf(w_branch_ret[0]), bf(w_out[0]),
    )
    ffn1 = (row(ffn1_norm[0]), bf(ffn1_w_in[0]), bf(ffn1_w_out[0]), row(final_norm))
    ffn2 = (row(ffn2_norm[0]), bf(ffn2_w_in[0]), bf(ffn2_w_out[0]), row(final_norm))

    inv = 1.0 / (ROPE_BASE ** jnp.linspace(0.0, 1.0, HEAD_DIM // 2, dtype=F32))
    inv2 = jnp.concatenate([inv, inv]).reshape(1, LANES)

    pad = SCAN_CHUNK - N_META
    meta_block = jnp.concatenate([jnp.zeros((pad, d), F32), meta_tokens.astype(F32)], axis=0)
    cos_m, sin_m = _rope_tables(inv2, SCAN_CHUNK, -pad)
    zero_state = jnp.zeros((HEADS, HEAD_DIM, HEAD_DIM), F32)
    h_meta = _ffn(meta_block, *ffn1, final=False)
    s_m, r_m, tail_m = _mixer(h_meta, cos_m, sin_m, params, zero_state, zero_state,
                              jnp.zeros((SUBLANES, 3 * hd), F32), batch=1, meta=True)

    cos_r, sin_r = _rope_tables(inv2, seq, N_META)
    h = _ffn(x.reshape(batch * seq, d), *ffn1, final=False)
    h = _mixer(h, cos_r, sin_r, params, s_m, r_m, tail_m, batch=batch, meta=False)
    h = _ffn(h, *ffn2, final=True)
    return h.reshape(batch, seq, d)
```

```python
import functools
import math

import jax
import jax.numpy as jnp
from jax import lax
from jax.experimental import pallas as pl
from jax.experimental.pallas import tpu as pltpu

F32 = jnp.float32
BF16 = jnp.bfloat16

EPS = 1e-6
N_META = 16
HEADS = 8
HEAD_DIM = 128
CONV_K = 4
ROPE_BASE = 10000.0
RET_LOG_GAMMA = tuple(math.log1p(-(2.0 ** (-5.0 - h))) for h in range(HEADS))

LANES = 128
SUBLANES = 8
V7X_VMEM_BYTES = 64 * 1024 * 1024

SCAN_CHUNK = LANES
SOLVE_BLOCK = 2 * SUBLANES
MIXER_ROWS = 2 * SCAN_CHUNK
FFN_ROWS = 512
FFN_COLS = 256
PIECE_COLS = 1024
ROPE_ROWS = 256


def _dot(a, b):
    return jnp.dot(a.astype(BF16), b.astype(BF16), preferred_element_type=F32)


def _dot_nt(a, b):
    return lax.dot_general(a.astype(BF16), b.astype(BF16), (((1,), (1,)), ((), ())),
                           preferred_element_type=F32)


def _dot_tn(a, b):
    return lax.dot_general(a.astype(BF16), b.astype(BF16), (((0,), (0,)), ((), ())),
                           preferred_element_type=F32)


def _sigmoid(x):
    return 1.0 / (1.0 + jnp.exp(-x))


def _softplus(x):
    return jnp.maximum(x, 0.0) + jnp.log1p(jnp.exp(-jnp.abs(x)))


def _rms(x, w):
    return x * lax.rsqrt(jnp.mean(x * x, axis=-1, keepdims=True) + EPS) * w


def _rope_body(inv_ref, cos_ref, sin_ref, *, pos0):
    rows = cos_ref.shape[0]
    row = lax.broadcasted_iota(jnp.int32, (rows, LANES), 0)
    lane = lax.broadcasted_iota(jnp.int32, (rows, LANES), 1)
    pos = (row + (pl.program_id(0) * rows + pos0)).astype(F32)
    ang = pos * inv_ref[...]
    cos_ref[...] = jnp.cos(ang)
    sin_ref[...] = jnp.where(lane < LANES // 2, -1.0, 1.0) * jnp.sin(ang)


def _rope_tables(inv2, n_rows, pos0):
    rows = min(ROPE_ROWS, n_rows)
    assert n_rows % rows == 0
    out = jax.ShapeDtypeStruct((n_rows, LANES), F32)
    spec = pl.BlockSpec((rows, LANES), lambda i: (i, 0))
    return pl.pallas_call(
        functools.partial(_rope_body, pos0=pos0),
        grid=(n_rows // rows,),
        in_specs=[pl.BlockSpec((1, LANES), lambda i: (0, 0))],
        out_specs=[spec, spec],
        out_shape=[out, out],
        name="rope_tables",
    )(inv2)


def _ffn_body(x_ref, nw_ref, win_ref, wout_ref, fw_ref, o_ref, *, final):
    d_ff = wout_ref.shape[0]
    x = x_ref[...]
    n = _rms(x, nw_ref[...]).astype(BF16)
    acc = jnp.zeros(x.shape, F32)
    for f in range(0, d_ff, FFN_COLS):
        gate = jnp.dot(n, win_ref[:, f:f + FFN_COLS], preferred_element_type=F32)
        up = jnp.dot(n, win_ref[:, d_ff + f:d_ff + f + FFN_COLS], preferred_element_type=F32)
        act = (gate * _sigmoid(gate) * up).astype(BF16)
        acc = acc + jnp.dot(act, wout_ref[f:f + FFN_COLS, :], preferred_element_type=F32)
    h = x + 0.5 * acc
    if final:
        h = _rms(h, fw_ref[...])
    o_ref[...] = h


def _resident(shape):
    return pl.BlockSpec(shape, lambda *_: (0,) * len(shape), pipeline_mode=pl.Buffered(1))


def _ffn(x2, norm_w, w_in, w_out, final_w, *, final):
    rows, d = x2.shape
    d_ff = w_out.shape[0]
    assert d_ff % FFN_COLS == 0
    tm = min(FFN_ROWS, rows)
    assert rows % tm == 0
    return pl.pallas_call(
        functools.partial(_ffn_body, final=final),
        grid=(rows // tm,),
        in_specs=[
            pl.BlockSpec((tm, d), lambda i: (i, 0)),
            _resident((1, d)),
            _resident((d, 2 * d_ff)),
            _resident((d_ff, d)),
            _resident((1, d)),
        ],
        out_specs=pl.BlockSpec((tm, d), lambda i: (i, 0)),
        out_shape=jax.ShapeDtypeStruct((rows, d), F32),
        compiler_params=pltpu.CompilerParams(
            dimension_semantics=("arbitrary",),
            vmem_limit_bytes=V7X_VMEM_BYTES * 3 // 4),
        name="ffn_final" if final else "ffn",
    )(x2, norm_w, w_in, w_out, final_w)


def _causal_masks():
    row = lax.broadcasted_iota(jnp.int32, (SCAN_CHUNK, SCAN_CHUNK), 0)
    col = lax.broadcasted_iota(jnp.int32, (SCAN_CHUNK, SCAN_CHUNK), 1)
    return row, col


def _unit_lower_solve(ns, rhs, row, col, fill):
    probs = range(len(ns))
    blk = (row // SOLVE_BLOCK) == (col // SOLVE_BLOCK)
    eye = (row == col).astype(F32)
    md = [jnp.where(blk, -n, 0.0) for n in ns]
    lo = [jnp.where(blk, 0.0, n) for n in ns]
    ss = [eye + m for m in md]
    mks = [_dot(m, m) for m in md]
    fill()
    k = 2
    while 2 * k < SOLVE_BLOCK:
        rs = [_dot(jnp.concatenate([s, mk], axis=0), mk) for s, mk in zip(ss, mks)]
        fill()
        ss = [s + r[:SCAN_CHUNK] for s, r in zip(ss, rs)]
        mks = [r[SCAN_CHUNK:] for r in rs]
        k *= 2
    td = [s + _dot(s, mk) for s, mk in zip(ss, mks)]
    fill()
    pb = [_dot(td[p], jnp.concatenate([lo[p], rhs[p]], axis=1)) for p in probs]
    fill()
    pl_b = [x[:, :SCAN_CHUNK].astype(BF16) for x in pb]
    bb = [x[:, SCAN_CHUNK:] for x in pb]
    xs = [[b[:SOLVE_BLOCK]] for b in bb]
    xs_b = [[x[0].astype(BF16)] for x in xs]
    for r0 in range(SOLVE_BLOCK, SCAN_CHUNK, SOLVE_BLOCK):
        rows = slice(r0, r0 + SOLVE_BLOCK)
        for p in probs:
            upd = jnp.dot(pl_b[p][rows, :r0], jnp.concatenate(xs_b[p], axis=0),
                          preferred_element_type=F32)
            xs[p].append(bb[p][rows] - upd)
            xs_b[p].append(xs[p][-1].astype(BF16))
        fill()
    return [jnp.concatenate(x, axis=0) for x in xs]


def _mixer_body(h_ref, cos_ref, sin_ref, mixw_ref, wfront_ref, wz_ref, wrg_ref, wga_ref, wgb_ref,
                cw_ref, alog_ref, dtb_ref, gn_ref, rn_ref, wbg_ref, wbr_ref, wout_ref, ltri_ref,
                s0_ref, r0_ref, tail0_ref, *rest, meta):
    if meta:
        s_out, r_out, tail_out = rest[:3]
        scratch = rest[3:]
    else:
        (out_ref,) = rest[:1]
        scratch = rest[1:]
    (cbuf, nbuf, pz, prg, pga, pgb, ya, yb, s_ref, r_ref, dr_ref, xi_ref, zeta_ref) = scratch
    hd = HEADS * HEAD_DIM
    conv_cols = 3 * hd
    ba_cols = slice(conv_cols, conv_cols + LANES)
    rq0, rk0, rv0 = conv_cols + LANES, conv_cols + LANES + hd, conv_cols + LANES + 2 * hd

    tt = h_ref.shape[0]
    n_chunks = tt // SCAN_CHUNK
    first_tile = pl.program_id(1) == 0
    row, col = _causal_masks()
    causal = row >= col
    strict = row > col
    ltri = ltri_ref[...]

    @pl.when(jnp.logical_and(pl.program_id(0) == 0, first_tile))
    def _init_tables():
        rowf = row.astype(F32)
        dist = jnp.where(causal, (row - col).astype(F32), 0.0)
        for h in range(HEADS):
            lg = RET_LOG_GAMMA[h]
            dr_ref[h] = jnp.where(causal, jnp.exp(dist * lg), 0.0)
            xi_ref[h] = jnp.exp((rowf + 1.0) * lg)
            zeta_ref[h] = jnp.exp((SCAN_CHUNK - 1.0 - rowf) * lg)

    @pl.when(first_tile)
    def _init_state():
        s_ref[...] = s0_ref[...]
        r_ref[...] = r0_ref[...]
        cbuf[0:SUBLANES, :conv_cols] = tail0_ref[...]

    nbuf[...] = _rms(h_ref[...], mixw_ref[...]).astype(BF16)
    gdn_cols = conv_cols + LANES
    front_cols = wfront_ref.shape[1]
    chunk_rows = lambda c: slice(c * SCAN_CHUNK, (c + 1) * SCAN_CHUNK)

    def front_piece(c, c0, c1):
        cbuf[SUBLANES + c * SCAN_CHUNK:SUBLANES + (c + 1) * SCAN_CHUNK, c0:c1] = jnp.dot(
            nbuf[chunk_rows(c), :], wfront_ref[:, c0:c1], preferred_element_type=F32)

    def gate_piece(w_ref, dst, c):
        dst[chunk_rows(c), :] = jnp.dot(nbuf[chunk_rows(c), :], w_ref[...], preferred_element_type=F32)

    def branch_piece(y_ref, w_ref, dst, c):
        dst[chunk_rows(c), :] = jnp.dot(y_ref[chunk_rows(c), :], w_ref[...], preferred_element_type=F32)

    def out_piece(c):
        rows = chunk_rows(c)
        merged = _sigmoid(pga[rows, :]) * pz[rows, :] + _sigmoid(pgb[rows, :]) * prg[rows, :]
        out_ref[rows, :] = h_ref[rows, :] + jnp.dot(merged.astype(BF16), wout_ref[...],
                                                    preferred_element_type=F32)

    deferred = []

    def defer(tag, fn, *args):
        deferred.append((tag, functools.partial(fn, *args)))

    def fill():
        if deferred:
            deferred.pop(0)[1]()

    def need(tag):
        while any(t == tag for t, _ in deferred):
            fill()

    def pieces(c0, c1):
        return [(a, min(a + PIECE_COLS, c1)) for a in range(c0, c1, PIECE_COLS)]

    front_piece(0, 0, gdn_cols)

    def conv_silu(r0, c0):
        cs = slice(c0, c0 + HEAD_DIM)
        hist = cbuf[r0:r0 + SUBLANES + SCAN_CHUNK, cs]
        y = cw_ref[CONV_K - 1:CONV_K, cs] * hist[SUBLANES:]
        for back in range(1, CONV_K):
            tap = CONV_K - 1 - back
            y = y + cw_ref[tap:tap + 1, cs] * pltpu.roll(hist, back, axis=0)[SUBLANES:]
        return y * _sigmoid(y)

    def l2norm(t):
        return t * lax.rsqrt(jnp.sum(t * t, axis=-1, keepdims=True) + EPS)

    def lane_bcast(a, lane):
        return jnp.broadcast_to(a[:, lane:lane + 1], (SCAN_CHUNK, LANES))

    def rotary(t, cos, sin):
        return t * cos + pltpu.roll(t, LANES // 2, axis=1) * sin

    qk_scale = HEAD_DIM ** -0.5
    heads = range(HEADS)
    hsl = [slice(h * HEAD_DIM, (h + 1) * HEAD_DIM) for h in heads]
    for c in range(n_chunks):
        r0 = c * SCAN_CHUNK
        rows = chunk_rows(c)
        crows = slice(SUBLANES + r0, SUBLANES + r0 + SCAN_CHUNK)
        for c0, c1 in pieces(gdn_cols, front_cols):
            defer(("ret", c), front_piece, c, c0, c1)
        if c + 1 < n_chunks:
            for c0, c1 in pieces(0, gdn_cols):
                defer(("gdn", c + 1), front_piece, c + 1, c0, c1)
        if not meta:
            defer(("zrg", c), gate_piece, wz_ref, pz, c)
            defer(("zrg", c), gate_piece, wrg_ref, prg, c)
            defer(("gab", c), gate_piece, wga_ref, pga, c)
            defer(("gab", c), gate_piece, wgb_ref, pgb, c)
            if c > 0:
                defer(("out", c - 1), branch_piece, ya, wbg_ref, pz, c - 1)
                defer(("out", c - 1), branch_piece, yb, wbr_ref, prg, c - 1)
                defer(("out", c - 1), out_piece, c - 1)

        ba = cbuf[crows, ba_cols]
        beta_c = _sigmoid(ba)
        g_c = -jnp.exp(alog_ref[...]) * _softplus(ba + dtb_ref[...])
        g_hi = g_c.astype(BF16)
        g_r1 = g_c - g_hi.astype(F32)
        g_mid = g_r1.astype(BF16)
        g_lo = (g_r1 - g_mid.astype(F32)).astype(BF16)
        cs_dot = functools.partial(jnp.dot, preferred_element_type=F32)
        gc = cs_dot(ltri, g_hi) + (cs_dot(ltri, g_mid) + cs_dot(ltri, g_lo))
        gc_t = gc.T

        q = [l2norm(conv_silu(r0, h * HEAD_DIM)) * qk_scale for h in heads]
        k = [l2norm(conv_silu(r0, (HEADS + h) * HEAD_DIM)) for h in heads]
        v = [conv_silu(r0, (2 * HEADS + h) * HEAD_DIM) for h in heads]
        gcol = [lane_bcast(gc, HEADS + h) for h in heads]
        grow = [gc_t[HEADS + h:HEADS + h + 1, :] for h in heads]
        bcol = [lane_bcast(beta_c, h) for h in heads]
        decay = [jnp.where(causal, jnp.exp(jnp.where(causal, gcol[h] - grow[h], 0.0)), 0.0)
                 for h in heads]
        kq = [_dot_nt(jnp.concatenate([k[h], q[h]], axis=0), k[h]) for h in heads]
        fill()
        a_low = [jnp.where(strict, bcol[h] * kq[h][:SCAN_CHUNK] * decay[h], 0.0) for h in heads]
        qk = [kq[h][SCAN_CHUNK:] * decay[h] for h in heads]
        eg = [jnp.exp(gcol[h]) for h in heads]
        uw = _unit_lower_solve(
            a_low, [jnp.concatenate([bcol[h] * v[h], bcol[h] * eg[h] * k[h]], axis=1) for h in heads],
            row, col, fill)
        g_last = [gcol[h][SCAN_CHUNK - 1:SCAN_CHUNK, :] for h in heads]
        k_dec = [k[h] * jnp.exp(g_last[h] - gcol[h]) for h in heads]

        need(("ret", c))
        cos = cos_ref[rows, :]
        sin = sin_ref[rows, :]
        front = lambda c0, h: cbuf[crows, c0 + h * HEAD_DIM:c0 + (h + 1) * HEAD_DIM]
        rq = [rotary(front(rq0, h), cos, sin) for h in heads]
        rk = [rotary(front(rk0, h), cos, sin) * qk_scale for h in heads]
        rv = [front(rv0, h) for h in heads]
        if not meta:
            scores = [_dot_nt(rq[h], rk[h]) * dr_ref[h] for h in heads]
            fill()

        state = [s_ref[h] for h in heads]
        rstate = [r_ref[h] for h in heads]
        wq_s = [_dot(jnp.concatenate([uw[h][:, HEAD_DIM:], q[h] * eg[h]], axis=0), state[h])
                for h in heads]
        fill()
        rdelta = [_dot_tn(rk[h] * zeta_ref[h], rv[h]) for h in heads]
        v_new = [uw[h][:, :HEAD_DIM] - wq_s[h][:SCAN_CHUNK] for h in heads]
        sdelta = [_dot_tn(k_dec[h], v_new[h]) for h in heads]
        for h in heads:
            s_ref[h] = state[h] * jnp.exp(g_last[h]) + sdelta[h]
            r_ref[h] = rstate[h] * math.exp(SCAN_CHUNK * RET_LOG_GAMMA[h]) + rdelta[h]
        if not meta:
            need(("zrg", c))
            o = [wq_s[h][SCAN_CHUNK:] + _dot(qk[h], v_new[h]) for h in heads]
            ob = [_dot(jnp.concatenate([scores[h], rq[h] * xi_ref[h]], axis=1),
                       jnp.concatenate([rv[h], rstate[h]], axis=0)) for h in heads]
            for h in heads:
                oh = o[h] * lax.rsqrt(jnp.mean(o[h] * o[h], axis=-1, keepdims=True) + EPS) * gn_ref[...]
                z = pz[rows, hsl[h]]
                ya[rows, hsl[h]] = (oh * (z * _sigmoid(z))).astype(BF16)
                mu = jnp.mean(ob[h], axis=-1, keepdims=True)
                cen = ob[h] - mu
                var = jnp.mean(cen * cen, axis=-1, keepdims=True)
                obn = cen * lax.rsqrt(var + EPS) * rn_ref[:, hsl[h]]
                rg = prg[rows, hsl[h]]
                yb[rows, hsl[h]] = (rg * _sigmoid(rg) * obn).astype(BF16)
        if c + 1 < n_chunks:
            need(("gdn", c + 1))

    cbuf[0:SUBLANES, :conv_cols] = cbuf[tt:tt + SUBLANES, :conv_cols]

    if meta:
        s_out[...] = s_ref[...]
        r_out[...] = r_ref[...]
        tail_out[...] = cbuf[0:SUBLANES, :conv_cols]
    else:
        while deferred:
            fill()
        last = n_chunks - 1
        branch_piece(ya, wbg_ref, pz, last)
        branch_piece(yb, wbr_ref, prg, last)
        out_piece(last)


def _mixer(h2, cos, sin, params, s0, r0, tail0, *, batch, meta):
    rows, d = h2.shape
    seq = rows // batch
    tt = SCAN_CHUNK if meta else MIXER_ROWS
    assert seq % tt == 0
    nt = seq // tt
    qkv_cols = 3 * HEADS * HEAD_DIM
    state_shape = (HEADS, HEAD_DIM, HEAD_DIM)
    in_specs = [
        pl.BlockSpec((tt, d), lambda b, i: (b * nt + i, 0)),
        pl.BlockSpec((tt, LANES), lambda b, i: (i, 0)),
        pl.BlockSpec((tt, LANES), lambda b, i: (i, 0)),
    ] + [_resident(p.shape) for p in params] + [
        _resident(state_shape), _resident(state_shape), _resident((SUBLANES, qkv_cols))]
    if meta:
        out_shape = [jax.ShapeDtypeStruct(state_shape, F32), jax.ShapeDtypeStruct(state_shape, F32),
                     jax.ShapeDtypeStruct((SUBLANES, qkv_cols), F32)]
        out_specs = [pl.BlockSpec(state_shape, lambda b, i: (0, 0, 0)),
                     pl.BlockSpec(state_shape, lambda b, i: (0, 0, 0)),
                     pl.BlockSpec((SUBLANES, qkv_cols), lambda b, i: (0, 0))]
    else:
        out_shape = jax.ShapeDtypeStruct((rows, d), F32)
        out_specs = pl.BlockSpec((tt, d), lambda b, i: (b * nt + i, 0))
    act = lambda dt: pltpu.VMEM((tt, d), dt)
    table = pltpu.VMEM(state_shape, F32)
    front_cols = params[1].shape[1]
    scratch = [pltpu.VMEM((SUBLANES + tt, front_cols), F32), act(BF16),
               act(F32), act(F32), act(F32), act(F32),
               act(BF16), act(BF16), table, table, table, table, table]
    return pl.pallas_call(
        functools.partial(_mixer_body, meta=meta),
        grid=(batch, nt),
        in_specs=in_specs,
        out_specs=out_specs,
        out_shape=out_shape,
        scratch_shapes=scratch,
        compiler_params=pltpu.CompilerParams(
            dimension_semantics=("arbitrary", "arbitrary"),
            vmem_limit_bytes=V7X_VMEM_BYTES * 7 // 8),
        name="mixer_meta" if meta else "mixer",
    )(h2, cos, sin, *params, s0, r0, tail0)


def _deinterleave_heads(w):
    d_in = w.shape[0]
    perm = jnp.concatenate([jnp.arange(0, HEAD_DIM, 2), jnp.arange(1, HEAD_DIM, 2)])
    return w.reshape(d_in, HEADS, HEAD_DIM)[:, :, perm].reshape(d_in, HEADS * HEAD_DIM)


def kernel(x, meta_tokens, ffn1_norm, ffn1_w_in, ffn1_w_out, mix_norm, w_in, gdn_conv_w, gdn_a_log, gdn_dt_bias, gdn_out_norm, ret_out_norm, w_branch_gdn, w_branch_ret, w_out, ffn2_norm, ffn2_w_in, ffn2_w_out, final_norm):
    batch, seq, d = x.shape
    assert ffn1_norm.shape[0] == 1, "single-layer stack only"
    assert meta_tokens.shape[0] == N_META and gdn_conv_w.shape[1] == CONV_K
    hd = HEADS * HEAD_DIM
    row = lambda v: v.reshape(1, -1).astype(F32)

    w = w_in[0]
    bounds = [0, 3 * hd, 4 * hd, 4 * hd + HEADS, 4 * hd + 2 * HEADS]
    for _ in range(6):
        bounds.append(bounds[-1] + hd)
    assert bounds[-1] == w.shape[1]
    piece = lambda j: w[:, bounds[j]:bounds[j + 1]]
    w_ba = jnp.zeros((d, LANES), F32).at[:, :2 * HEADS].set(jnp.concatenate([piece(2), piece(3)], axis=1))
    gate_vec = lambda v: jnp.zeros((1, LANES), F32).at[0, HEADS:2 * HEADS].set(v[0].astype(F32))
    conv_w = jnp.zeros((SUBLANES, 3 * hd), F32).at[:CONV_K].set(gdn_conv_w[0].astype(F32))
    bf = lambda a: a.astype(BF16)
    w_front = jnp.concatenate(
        [piece(0), w_ba, _deinterleave_heads(piece(4)), _deinterleave_heads(piece(5)), piece(6)], axis=1)
    params = (
        row(mix_norm[0]), bf(w_front), bf(piece(1)), bf(piece(7)), bf(piece(8)), bf(piece(9)),
        conv_w, gate_vec(gdn_a_log), gate_vec(gdn_dt_bias), row(gdn_out_norm[0]), row(ret_out_norm[0]),
        bf(w_branch_gdn[0]), bf(w_branch_ret[0]), bf(w_out[0]),
        jnp.tril(jnp.ones((SCAN_CHUNK, SCAN_CHUNK), BF16)),
    )
    ffn1 = (row(ffn1_norm[0]), bf(ffn1_w_in[0]), bf(ffn1_w_out[0]), row(final_norm))
    ffn2 = (row(ffn2_norm[0]), bf(ffn2_w_in[0]), bf(ffn2_w_out[0]), row(final_norm))

    inv = 1.0 / (ROPE_BASE ** jnp.linspace(0.0, 1.0, HEAD_DIM // 2, dtype=F32))
    inv2 = jnp.concatenate([inv, inv]).reshape(1, LANES)

    pad = SCAN_CHUNK - N_META
    meta_block = jnp.concatenate([jnp.zeros((pad, d), F32), meta_tokens.astype(F32)], axis=0)
    cos_m, sin_m = _rope_tables(inv2, SCAN_CHUNK, -pad)
    zero_state = jnp.zeros((HEADS, HEAD_DIM, HEAD_DIM), F32)
    h_meta = _ffn(meta_block, *ffn1, final=False)
    s_m, r_m, tail_m = _mixer(h_meta, cos_m, sin_m, params, zero_state, zero_state,
                              jnp.zeros((SUBLANES, 3 * hd), F32), batch=1, meta=True)

    cos_r, sin_r = _rope_tables(inv2, seq, N_META)
    h = _ffn(x.reshape(batch * seq, d), *ffn1, final=False)
    h = _mixer(h, cos_r, sin_r, params, s_m, r_m, tail_m, batch=batch, meta=False)
    h = _ffn(h, *ffn2, final=True)
    return h.reshape(batch, seq, d)
```

```python
import functools
import math

import jax
import jax.numpy as jnp
from jax import lax
from jax.experimental import pallas as pl
from jax.experimental.pallas import tpu as pltpu

F32 = jnp.float32
BF16 = jnp.bfloat16

EPS = 1e-6
N_META = 16
HEADS = 8
HEAD_DIM = 128
CONV_K = 4
ROPE_BASE = 10000.0
RET_LOG_GAMMA = tuple(math.log1p(-(2.0 ** (-5.0 - h))) for h in range(HEADS))
MASKED_LOG = -1e30

LANES = 128
SUBLANES = 8
V7X_VMEM_BYTES = 64 * 1024 * 1024

SCAN_CHUNK = LANES
SOLVE_BLOCK = 2 * SUBLANES
MIXER_ROWS = 2 * SCAN_CHUNK
FFN_ROWS = 512
FFN_COLS = 256
FILL_COLS = 256
ROPE_ROWS = 256


def _dot(a, b):
    return jnp.dot(a.astype(BF16), b.astype(BF16), preferred_element_type=F32)


def _dot_nt(a, b):
    return lax.dot_general(a.astype(BF16), b.astype(BF16), (((1,), (1,)), ((), ())),
                           preferred_element_type=F32)


def _dot_tn(a, b):
    return lax.dot_general(a.astype(BF16), b.astype(BF16), (((0,), (0,)), ((), ())),
                           preferred_element_type=F32)


def _sigmoid(x):
    return 1.0 / (1.0 + jnp.exp(-x))


def _softplus(x):
    return jnp.maximum(x, 0.0) + jnp.log1p(jnp.exp(-jnp.abs(x)))


def _rms(x, w):
    return x * lax.rsqrt(jnp.mean(x * x, axis=-1, keepdims=True) + EPS) * w


def _rope_body(inv_ref, cos_ref, sin_ref, *, pos0):
    rows = cos_ref.shape[0]
    row = lax.broadcasted_iota(jnp.int32, (rows, LANES), 0)
    lane = lax.broadcasted_iota(jnp.int32, (rows, LANES), 1)
    pos = (row + (pl.program_id(0) * rows + pos0)).astype(F32)
    ang = pos * inv_ref[...]
    cos_ref[...] = jnp.cos(ang)
    sin_ref[...] = jnp.where(lane < LANES // 2, -1.0, 1.0) * jnp.sin(ang)


def _rope_tables(inv2, n_rows, pos0):
    rows = min(ROPE_ROWS, n_rows)
    assert n_rows % rows == 0
    out = jax.ShapeDtypeStruct((n_rows, LANES), F32)
    spec = pl.BlockSpec((rows, LANES), lambda i: (i, 0))
    return pl.pallas_call(
        functools.partial(_rope_body, pos0=pos0),
        grid=(n_rows // rows,),
        in_specs=[pl.BlockSpec((1, LANES), lambda i: (0, 0))],
        out_specs=[spec, spec],
        out_shape=[out, out],
        name="rope_tables",
    )(inv2)


def _ffn_body(x_ref, nw_ref, win_ref, wout_ref, fw_ref, o_ref, *, final):
    d_ff = wout_ref.shape[0]
    x = x_ref[...]
    n = _rms(x, nw_ref[...]).astype(BF16)
    acc = jnp.zeros(x.shape, F32)
    for f in range(0, d_ff, FFN_COLS):
        gate = jnp.dot(n, win_ref[:, f:f + FFN_COLS], preferred_element_type=F32)
        up = jnp.dot(n, win_ref[:, d_ff + f:d_ff + f + FFN_COLS], preferred_element_type=F32)
        act = (gate * _sigmoid(gate) * up).astype(BF16)
        acc = acc + jnp.dot(act, wout_ref[f:f + FFN_COLS, :], preferred_element_type=F32)
    h = x + 0.5 * acc
    if final:
        h = _rms(h, fw_ref[...])
    o_ref[...] = h


def _resident(shape):
    return pl.BlockSpec(shape, lambda *_: (0,) * len(shape), pipeline_mode=pl.Buffered(1))


def _ffn(x2, norm_w, w_in, w_out, final_w, *, final):
    rows, d = x2.shape
    d_ff = w_out.shape[0]
    assert d_ff % FFN_COLS == 0
    tm = min(FFN_ROWS, rows)
    assert rows % tm == 0
    return pl.pallas_call(
        functools.partial(_ffn_body, final=final),
        grid=(rows // tm,),
        in_specs=[
            pl.BlockSpec((tm, d), lambda i: (i, 0)),
            _resident((1, d)),
            _resident((d, 2 * d_ff)),
            _resident((d_ff, d)),
            _resident((1, d)),
        ],
        out_specs=pl.BlockSpec((tm, d), lambda i: (i, 0)),
        out_shape=jax.ShapeDtypeStruct((rows, d), F32),
        compiler_params=pltpu.CompilerParams(
            dimension_semantics=("arbitrary",),
            vmem_limit_bytes=V7X_VMEM_BYTES * 3 // 4),
        name="ffn_final" if final else "ffn",
    )(x2, norm_w, w_in, w_out, final_w)


def _causal_masks():
    row = lax.broadcasted_iota(jnp.int32, (SCAN_CHUNK, SCAN_CHUNK), 0)
    col = lax.broadcasted_iota(jnp.int32, (SCAN_CHUNK, SCAN_CHUNK), 1)
    return row, col


def _unit_lower_solve(ns, rhs, row, col, fill):
    probs = range(len(ns))
    blk = (row // SOLVE_BLOCK) == (col // SOLVE_BLOCK)
    eye = (row == col).astype(F32)
    md = [jnp.where(blk, -n, 0.0) for n in ns]
    lo = [jnp.where(blk, 0.0, n) for n in ns]
    ss = [eye + m for m in md]
    mks = [_dot(m, m) for m in md]
    fill()
    k = 2
    while 2 * k < SOLVE_BLOCK:
        rs = [_dot(jnp.concatenate([s, mk], axis=0), mk) for s, mk in zip(ss, mks)]
        fill()
        ss = [s + r[:SCAN_CHUNK] for s, r in zip(ss, rs)]
        mks = [r[SCAN_CHUNK:] for r in rs]
        k *= 2
    td = [s + _dot(s, mk) for s, mk in zip(ss, mks)]
    fill()
    pb = [_dot(td[p], jnp.concatenate([lo[p], rhs[p]], axis=1)) for p in probs]
    fill()
    pl_b = [x[:, :SCAN_CHUNK].astype(BF16) for x in pb]
    bb = [x[:, SCAN_CHUNK:] for x in pb]
    xs = [[b[:SOLVE_BLOCK]] for b in bb]
    xs_b = [[x[0].astype(BF16)] for x in xs]
    for r0 in range(SOLVE_BLOCK, SCAN_CHUNK, SOLVE_BLOCK):
        rows = slice(r0, r0 + SOLVE_BLOCK)
        for p in probs:
            upd = jnp.dot(pl_b[p][rows, :r0], jnp.concatenate(xs_b[p], axis=0),
                          preferred_element_type=F32)
            xs[p].append(bb[p][rows] - upd)
            xs_b[p].append(xs[p][-1].astype(BF16))
        fill()
    return [jnp.concatenate(x, axis=0) for x in xs]


def _mixer_body(h_ref, cos_ref, sin_ref, mixw_ref, wqkv_ref, wz_ref, wba_ref, wrq_ref,
                wrk_ref, wrv_ref, wrg_ref, wga_ref, wgb_ref, cw_ref, alog_ref, dtb_ref,
                gn_ref, rn_ref, wbg_ref, wbr_ref, wout_ref, ltri_ref, s0_ref, r0_ref, tail0_ref,
                *rest, meta):
    if meta:
        s_out, r_out, tail_out = rest[:3]
        scratch = rest[3:]
    else:
        (out_ref,) = rest[:1]
        scratch = rest[1:]
    (cbuf, nbuf, pz, prq, prk, prv, prg, pga, pgb, ya, yb,
     s_ref, r_ref, dr_ref, xi_ref, zeta_ref) = scratch

    tt = h_ref.shape[0]
    n_chunks = tt // SCAN_CHUNK
    first_tile = pl.program_id(1) == 0
    row, col = _causal_masks()
    causal = row >= col
    strict = row > col
    ltri = ltri_ref[...]

    @pl.when(jnp.logical_and(pl.program_id(0) == 0, first_tile))
    def _init_tables():
        rowf = row.astype(F32)
        dist = jnp.where(causal, (row - col).astype(F32), 0.0)
        for h in range(HEADS):
            lg = RET_LOG_GAMMA[h]
            dr_ref[h] = jnp.where(causal, jnp.exp(dist * lg), 0.0)
            xi_ref[h] = jnp.exp((rowf + 1.0) * lg)
            zeta_ref[h] = jnp.exp((SCAN_CHUNK - 1.0 - rowf) * lg)

    @pl.when(first_tile)
    def _init_state():
        s_ref[...] = s0_ref[...]
        r_ref[...] = r0_ref[...]
        cbuf[0:SUBLANES, :] = tail0_ref[...]

    nbuf[...] = _rms(h_ref[...], mixw_ref[...]).astype(BF16)
    proj = lambda w_ref: jnp.dot(nbuf[...], w_ref[...], preferred_element_type=F32)
    cbuf[SUBLANES:SUBLANES + tt, :] = proj(wqkv_ref)
    ba = proj(wba_ref)
    prq[...] = proj(wrq_ref)
    prk[...] = proj(wrk_ref)
    prv[...] = proj(wrv_ref)

    def proj_piece(w_ref, dst, c0):
        cs = slice(c0, c0 + FILL_COLS)
        dst[:, cs] = jnp.dot(nbuf[...], w_ref[:, cs], preferred_element_type=F32)

    deferred = []
    if not meta:
        for w_ref, dst in ((wz_ref, pz), (wrg_ref, prg), (wga_ref, pga), (wgb_ref, pgb)):
            deferred += [functools.partial(proj_piece, w_ref, dst, c0)
                         for c0 in range(0, w_ref.shape[1], FILL_COLS)]
    n_out_gate_pieces = len(deferred) // 2

    def fill():
        if deferred:
            deferred.pop(0)()

    def fill_until(remaining):
        while len(deferred) > remaining:
            fill()

    beta_all = _sigmoid(ba)
    g_all = -jnp.exp(alog_ref[...]) * _softplus(ba + dtb_ref[...])

    def conv_silu(r0, c0):
        cs = slice(c0, c0 + HEAD_DIM)
        hist = cbuf[r0:r0 + SUBLANES + SCAN_CHUNK, cs]
        y = cw_ref[CONV_K - 1:CONV_K, cs] * hist[SUBLANES:]
        for back in range(1, CONV_K):
            tap = CONV_K - 1 - back
            y = y + cw_ref[tap:tap + 1, cs] * pltpu.roll(hist, back, axis=0)[SUBLANES:]
        return y * _sigmoid(y)

    def l2norm(t):
        return t * lax.rsqrt(jnp.sum(t * t, axis=-1, keepdims=True) + EPS)

    def lane_bcast(a, lane):
        return jnp.broadcast_to(a[:, lane:lane + 1], (SCAN_CHUNK, LANES))

    def rotary(t, cos, sin):
        return t * cos + pltpu.roll(t, LANES // 2, axis=1) * sin

    qk_scale = HEAD_DIM ** -0.5
    for c in range(n_chunks):
        r0 = c * SCAN_CHUNK
        rows = slice(r0, r0 + SCAN_CHUNK)
        g_c = g_all[rows]
        g_hi = g_c.astype(BF16)
        g_r1 = g_c - g_hi.astype(F32)
        g_mid = g_r1.astype(BF16)
        g_lo = (g_r1 - g_mid.astype(F32)).astype(BF16)
        cs_dot = functools.partial(jnp.dot, preferred_element_type=F32)
        gc = cs_dot(ltri, g_hi) + (cs_dot(ltri, g_mid) + cs_dot(ltri, g_lo))
        gc_t = gc.T
        beta_c = beta_all[rows]
        cos = cos_ref[rows, :]
        sin = sin_ref[rows, :]

        heads = range(HEADS)
        hsl = [slice(h * HEAD_DIM, (h + 1) * HEAD_DIM) for h in heads]

        q = [l2norm(conv_silu(r0, h * HEAD_DIM)) * qk_scale for h in heads]
        k = [l2norm(conv_silu(r0, (HEADS + h) * HEAD_DIM)) for h in heads]
        v = [conv_silu(r0, (2 * HEADS + h) * HEAD_DIM) for h in heads]
        gcol = [lane_bcast(gc, HEADS + h) for h in heads]
        grow = [gc_t[HEADS + h:HEADS + h + 1, :] for h in heads]
        bcol = [lane_bcast(beta_c, h) for h in heads]
        decay = [jnp.exp(jnp.where(causal, gcol[h] - grow[h], MASKED_LOG)) for h in heads]
        kq = [_dot_nt(jnp.concatenate([k[h], q[h]], axis=0), k[h]) for h in heads]
        rq = [rotary(prq[rows, hsl[h]], cos, sin) for h in heads]
        rk = [rotary(prk[rows, hsl[h]], cos, sin) * qk_scale for h in heads]
        rv = [prv[rows, hsl[h]] for h in heads]
        if not meta:
            scores = [_dot_nt(rq[h], rk[h]) * dr_ref[h] for h in heads]
        a_low = [jnp.where(strict, bcol[h] * kq[h][:SCAN_CHUNK] * decay[h], 0.0) for h in heads]
        qk = [kq[h][SCAN_CHUNK:] * decay[h] for h in heads]
        eg = [jnp.exp(gcol[h]) for h in heads]
        uw = _unit_lower_solve(
            a_low, [jnp.concatenate([bcol[h] * v[h], bcol[h] * eg[h] * k[h]], axis=1) for h in heads],
            row, col, fill)
        g_last = [gcol[h][SCAN_CHUNK - 1:SCAN_CHUNK, :] for h in heads]
        k_dec = [k[h] * jnp.exp(g_last[h] - gcol[h]) for h in heads]

        state = [s_ref[h] for h in heads]
        rstate = [r_ref[h] for h in heads]
        w_s = [_dot(uw[h][:, HEAD_DIM:], state[h]) for h in heads]
        rdelta = [_dot_tn(rk[h] * zeta_ref[h], rv[h]) for h in heads]
        v_new = [uw[h][:, :HEAD_DIM] - w_s[h] for h in heads]
        sdelta = [_dot_tn(k_dec[h], v_new[h]) for h in heads]
        for h in heads:
            s_ref[h] = state[h] * jnp.exp(g_last[h]) + sdelta[h]
            r_ref[h] = rstate[h] * math.exp(SCAN_CHUNK * RET_LOG_GAMMA[h]) + rdelta[h]
        if not meta:
            fill_until(n_out_gate_pieces)
            o = [_dot(jnp.concatenate([qk[h], q[h] * eg[h]], axis=1),
                      jnp.concatenate([v_new[h], state[h]], axis=0)) for h in heads]
            ob = [_dot(jnp.concatenate([scores[h], rq[h] * xi_ref[h]], axis=1),
                       jnp.concatenate([rv[h], rstate[h]], axis=0)) for h in heads]
            for h in heads:
                oh = o[h] * lax.rsqrt(jnp.mean(o[h] * o[h], axis=-1, keepdims=True) + EPS) * gn_ref[...]
                z = pz[rows, hsl[h]]
                ya[rows, hsl[h]] = (oh * (z * _sigmoid(z))).astype(BF16)
                mu = jnp.mean(ob[h], axis=-1, keepdims=True)
                cen = ob[h] - mu
                var = jnp.mean(cen * cen, axis=-1, keepdims=True)
                obn = cen * lax.rsqrt(var + EPS) * rn_ref[:, hsl[h]]
                rg = prg[rows, hsl[h]]
                yb[rows, hsl[h]] = (rg * _sigmoid(rg) * obn).astype(BF16)

    cbuf[0:SUBLANES, :] = cbuf[tt:tt + SUBLANES, :]

    if meta:
        s_out[...] = s_ref[...]
        r_out[...] = r_ref[...]
        tail_out[...] = cbuf[0:SUBLANES, :]
    else:
        fill_until(0)
        merged = (_sigmoid(pga[...]) * jnp.dot(ya[...], wbg_ref[...], preferred_element_type=F32)
                  + _sigmoid(pgb[...]) * jnp.dot(yb[...], wbr_ref[...], preferred_element_type=F32))
        out_ref[...] = h_ref[...] + jnp.dot(merged.astype(BF16), wout_ref[...], preferred_element_type=F32)


def _mixer(h2, cos, sin, params, s0, r0, tail0, *, batch, meta):
    rows, d = h2.shape
    seq = rows // batch
    tt = SCAN_CHUNK if meta else MIXER_ROWS
    assert seq % tt == 0
    nt = seq // tt
    qkv_cols = 3 * HEADS * HEAD_DIM
    state_shape = (HEADS, HEAD_DIM, HEAD_DIM)
    in_specs = [
        pl.BlockSpec((tt, d), lambda b, i: (b * nt + i, 0)),
        pl.BlockSpec((tt, LANES), lambda b, i: (i, 0)),
        pl.BlockSpec((tt, LANES), lambda b, i: (i, 0)),
    ] + [_resident(p.shape) for p in params] + [
        _resident(state_shape), _resident(state_shape), _resident((SUBLANES, qkv_cols))]
    if meta:
        out_shape = [jax.ShapeDtypeStruct(state_shape, F32), jax.ShapeDtypeStruct(state_shape, F32),
                     jax.ShapeDtypeStruct((SUBLANES, qkv_cols), F32)]
        out_specs = [pl.BlockSpec(state_shape, lambda b, i: (0, 0, 0)),
                     pl.BlockSpec(state_shape, lambda b, i: (0, 0, 0)),
                     pl.BlockSpec((SUBLANES, qkv_cols), lambda b, i: (0, 0))]
    else:
        out_shape = jax.ShapeDtypeStruct((rows, d), F32)
        out_specs = pl.BlockSpec((tt, d), lambda b, i: (b * nt + i, 0))
    act = lambda dt: pltpu.VMEM((tt, d), dt)
    table = pltpu.VMEM(state_shape, F32)
    scratch = [pltpu.VMEM((SUBLANES + tt, qkv_cols), F32), act(BF16),
               act(F32), act(F32), act(F32), act(F32), act(F32), act(F32), act(F32),
               act(BF16), act(BF16), table, table, table, table, table]
    return pl.pallas_call(
        functools.partial(_mixer_body, meta=meta),
        grid=(batch, nt),
        in_specs=in_specs,
        out_specs=out_specs,
        out_shape=out_shape,
        scratch_shapes=scratch,
        compiler_params=pltpu.CompilerParams(
            dimension_semantics=("arbitrary", "arbitrary"),
            vmem_limit_bytes=V7X_VMEM_BYTES * 7 // 8),
        name="mixer_meta" if meta else "mixer",
    )(h2, cos, sin, *params, s0, r0, tail0)


def _deinterleave_heads(w):
    d_in = w.shape[0]
    perm = jnp.concatenate([jnp.arange(0, HEAD_DIM, 2), jnp.arange(1, HEAD_DIM, 2)])
    return w.reshape(d_in, HEADS, HEAD_DIM)[:, :, perm].reshape(d_in, HEADS * HEAD_DIM)


def kernel(x, meta_tokens, ffn1_norm, ffn1_w_in, ffn1_w_out, mix_norm, w_in, gdn_conv_w, gdn_a_log, gdn_dt_bias, gdn_out_norm, ret_out_norm, w_branch_gdn, w_branch_ret, w_out, ffn2_norm, ffn2_w_in, ffn2_w_out, final_norm):
    batch, seq, d = x.shape
    assert ffn1_norm.shape[0] == 1, "single-layer stack only"
    assert meta_tokens.shape[0] == N_META and gdn_conv_w.shape[1] == CONV_K
    hd = HEADS * HEAD_DIM
    row = lambda v: v.reshape(1, -1).astype(F32)

    w = w_in[0]
    bounds = [0, 3 * hd, 4 * hd, 4 * hd + HEADS, 4 * hd + 2 * HEADS]
    for _ in range(6):
        bounds.append(bounds[-1] + hd)
    assert bounds[-1] == w.shape[1]
    piece = lambda j: w[:, bounds[j]:bounds[j + 1]]
    w_ba = jnp.zeros((d, LANES), F32).at[:, :2 * HEADS].set(jnp.concatenate([piece(2), piece(3)], axis=1))
    gate_vec = lambda v: jnp.zeros((1, LANES), F32).at[0, HEADS:2 * HEADS].set(v[0].astype(F32))
    conv_w = jnp.zeros((SUBLANES, 3 * hd), F32).at[:CONV_K].set(gdn_conv_w[0].astype(F32))
    bf = lambda a: a.astype(BF16)
    params = (
        row(mix_norm[0]), bf(piece(0)), bf(piece(1)), bf(w_ba),
        bf(_deinterleave_heads(piece(4))), bf(_deinterleave_heads(piece(5))), bf(piece(6)), bf(piece(7)),
        bf(piece(8)), bf(piece(9)), conv_w, gate_vec(gdn_a_log), gate_vec(gdn_dt_bias),
        row(gdn_out_norm[0]), row(ret_out_norm[0]),
        bf(w_branch_gdn[0]), bf(w_branch_ret[0]), bf(w_out[0]),
        jnp.tril(jnp.ones((SCAN_CHUNK, SCAN_CHUNK), BF16)),
    )
    ffn1 = (row(ffn1_norm[0]), bf(ffn1_w_in[0]), bf(ffn1_w_out[0]), row(final_norm))
    ffn2 = (row(ffn2_norm[0]), bf(ffn2_w_in[0]), bf(ffn2_w_out[0]), row(final_norm))

    inv = 1.0 / (ROPE_BASE ** jnp.linspace(0.0, 1.0, HEAD_DIM // 2, dtype=F32))
    inv2 = jnp.concatenate([inv, inv]).reshape(1, LANES)

    pad = SCAN_CHUNK - N_META
    meta_block = jnp.concatenate([jnp.zeros((pad, d), F32), meta_tokens.astype(F32)], axis=0)
    cos_m, sin_m = _rope_tables(inv2, SCAN_CHUNK, -pad)
    zero_state = jnp.zeros((HEADS, HEAD_DIM, HEAD_DIM), F32)
    h_meta = _ffn(meta_block, *ffn1, final=False)
    s_m, r_m, tail_m = _mixer(h_meta, cos_m, sin_m, params, zero_state, zero_state,
                              jnp.zeros((SUBLANES, 3 * hd), F32), batch=1, meta=True)

    cos_r, sin_r = _rope_tables(inv2, seq, N_META)
    h = _ffn(x.reshape(batch * seq, d), *ffn1, final=False)
    h = _mixer(h, cos_r, sin_r, params, s_m, r_m, tail_m, batch=batch, meta=False)
    h = _ffn(h, *ffn2, final=True)
    return h.reshape(batch, seq, d)
```

```python
import functools
import math

import jax
import jax.numpy as jnp
from jax import lax
from jax.experimental import pallas as pl
from jax.experimental.pallas import tpu as pltpu

F32 = jnp.float32
BF16 = jnp.bfloat16

EPS = 1e-6
N_META = 16
HEADS = 8
HEAD_DIM = 128
CONV_K = 4
ROPE_BASE = 10000.0
RET_LOG_GAMMA = tuple(math.log1p(-(2.0 ** (-5.0 - h))) for h in range(HEADS))
MASKED_LOG = -1e30

LANES = 128
SUBLANES = 8
V7X_VMEM_BYTES = 64 * 1024 * 1024

SCAN_CHUNK = LANES
SOLVE_BLOCK = 2 * SUBLANES
MIXER_ROWS = 2 * SCAN_CHUNK
FFN_ROWS = 512
FFN_COLS = 256
FILL_COLS = 512
ROPE_ROWS = 256


def _dot(a, b):
    return jnp.dot(a.astype(BF16), b.astype(BF16), preferred_element_type=F32)


def _dot_nt(a, b):
    return lax.dot_general(a.astype(BF16), b.astype(BF16), (((1,), (1,)), ((), ())),
                           preferred_element_type=F32)


def _dot_tn(a, b):
    return lax.dot_general(a.astype(BF16), b.astype(BF16), (((0,), (0,)), ((), ())),
                           preferred_element_type=F32)


def _sigmoid(x):
    return 1.0 / (1.0 + jnp.exp(-x))


def _softplus(x):
    return jnp.maximum(x, 0.0) + jnp.log1p(jnp.exp(-jnp.abs(x)))


def _rms(x, w):
    return x * lax.rsqrt(jnp.mean(x * x, axis=-1, keepdims=True) + EPS) * w


def _rope_body(inv_ref, cos_ref, sin_ref, *, pos0):
    rows = cos_ref.shape[0]
    row = lax.broadcasted_iota(jnp.int32, (rows, LANES), 0)
    lane = lax.broadcasted_iota(jnp.int32, (rows, LANES), 1)
    pos = (row + (pl.program_id(0) * rows + pos0)).astype(F32)
    ang = pos * inv_ref[...]
    cos_ref[...] = jnp.cos(ang)
    sin_ref[...] = jnp.where(lane < LANES // 2, -1.0, 1.0) * jnp.sin(ang)


def _rope_tables(inv2, n_rows, pos0):
    rows = min(ROPE_ROWS, n_rows)
    assert n_rows % rows == 0
    out = jax.ShapeDtypeStruct((n_rows, LANES), F32)
    spec = pl.BlockSpec((rows, LANES), lambda i: (i, 0))
    return pl.pallas_call(
        functools.partial(_rope_body, pos0=pos0),
        grid=(n_rows // rows,),
        in_specs=[pl.BlockSpec((1, LANES), lambda i: (0, 0))],
        out_specs=[spec, spec],
        out_shape=[out, out],
        name="rope_tables",
    )(inv2)


def _ffn_body(x_ref, nw_ref, win_ref, wout_ref, fw_ref, o_ref, *, final):
    d_ff = wout_ref.shape[0]
    x = x_ref[...]
    n = _rms(x, nw_ref[...]).astype(BF16)
    acc = jnp.zeros(x.shape, F32)
    for f in range(0, d_ff, FFN_COLS):
        gate = jnp.dot(n, win_ref[:, f:f + FFN_COLS], preferred_element_type=F32)
        up = jnp.dot(n, win_ref[:, d_ff + f:d_ff + f + FFN_COLS], preferred_element_type=F32)
        act = (gate * _sigmoid(gate) * up).astype(BF16)
        acc = acc + jnp.dot(act, wout_ref[f:f + FFN_COLS, :], preferred_element_type=F32)
    h = x + 0.5 * acc
    if final:
        h = _rms(h, fw_ref[...])
    o_ref[...] = h


def _resident(shape):
    return pl.BlockSpec(shape, lambda *_: (0,) * len(shape), pipeline_mode=pl.Buffered(1))


def _ffn(x2, norm_w, w_in, w_out, final_w, *, final):
    rows, d = x2.shape
    d_ff = w_out.shape[0]
    assert d_ff % FFN_COLS == 0
    tm = min(FFN_ROWS, rows)
    assert rows % tm == 0
    return pl.pallas_call(
        functools.partial(_ffn_body, final=final),
        grid=(rows // tm,),
        in_specs=[
            pl.BlockSpec((tm, d), lambda i: (i, 0)),
            _resident((1, d)),
            _resident((d, 2 * d_ff)),
            _resident((d_ff, d)),
            _resident((1, d)),
        ],
        out_specs=pl.BlockSpec((tm, d), lambda i: (i, 0)),
        out_shape=jax.ShapeDtypeStruct((rows, d), F32),
        compiler_params=pltpu.CompilerParams(
            dimension_semantics=("arbitrary",),
            vmem_limit_bytes=V7X_VMEM_BYTES * 3 // 4),
        name="ffn_final" if final else "ffn",
    )(x2, norm_w, w_in, w_out, final_w)


def _causal_masks():
    row = lax.broadcasted_iota(jnp.int32, (SCAN_CHUNK, SCAN_CHUNK), 0)
    col = lax.broadcasted_iota(jnp.int32, (SCAN_CHUNK, SCAN_CHUNK), 1)
    return row, col


def _unit_lower_solve(ns, rhs, row, col, fill):
    probs = range(len(ns))
    blk = (row // SOLVE_BLOCK) == (col // SOLVE_BLOCK)
    eye = (row == col).astype(F32)
    md = [jnp.where(blk, -n, 0.0) for n in ns]
    lo = [jnp.where(blk, 0.0, n) for n in ns]
    ss = [eye + m for m in md]
    mks = [_dot(m, m) for m in md]
    fill()
    k = 2
    while 2 * k < SOLVE_BLOCK:
        rs = [_dot(jnp.concatenate([s, mk], axis=0), mk) for s, mk in zip(ss, mks)]
        fill()
        ss = [s + r[:SCAN_CHUNK] for s, r in zip(ss, rs)]
        mks = [r[SCAN_CHUNK:] for r in rs]
        k *= 2
    td = [s + _dot(s, mk) for s, mk in zip(ss, mks)]
    fill()
    pb = [_dot(td[p], jnp.concatenate([lo[p], rhs[p]], axis=1)) for p in probs]
    fill()
    pl_b = [x[:, :SCAN_CHUNK].astype(BF16) for x in pb]
    bb = [x[:, SCAN_CHUNK:] for x in pb]
    xs = [[b[:SOLVE_BLOCK]] for b in bb]
    xs_b = [[x[0].astype(BF16)] for x in xs]
    for r0 in range(SOLVE_BLOCK, SCAN_CHUNK, SOLVE_BLOCK):
        rows = slice(r0, r0 + SOLVE_BLOCK)
        for p in probs:
            upd = jnp.dot(pl_b[p][rows, :r0], jnp.concatenate(xs_b[p], axis=0),
                          preferred_element_type=F32)
            xs[p].append(bb[p][rows] - upd)
            xs_b[p].append(xs[p][-1].astype(BF16))
        fill()
    return [jnp.concatenate(x, axis=0) for x in xs]


def _mixer_body(h_ref, cos_ref, sin_ref, mixw_ref, wfront_ref, wz_ref, wrg_ref, wga_ref, wgb_ref,
                cw_ref, alog_ref, dtb_ref, gn_ref, rn_ref, wbg_ref, wbr_ref, wout_ref, ltri_ref,
                s0_ref, r0_ref, tail0_ref, *rest, meta):
    if meta:
        s_out, r_out, tail_out = rest[:3]
        scratch = rest[3:]
    else:
        (out_ref,) = rest[:1]
        scratch = rest[1:]
    (cbuf, nbuf, pz, prg, pga, pgb, ya, yb, s_ref, r_ref, dr_ref, xi_ref, zeta_ref) = scratch
    hd = HEADS * HEAD_DIM
    conv_cols = 3 * hd
    ba_cols = slice(conv_cols, conv_cols + LANES)
    rq0, rk0, rv0 = conv_cols + LANES, conv_cols + LANES + hd, conv_cols + LANES + 2 * hd

    tt = h_ref.shape[0]
    n_chunks = tt // SCAN_CHUNK
    first_tile = pl.program_id(1) == 0
    row, col = _causal_masks()
    causal = row >= col
    strict = row > col
    ltri = ltri_ref[...]

    @pl.when(jnp.logical_and(pl.program_id(0) == 0, first_tile))
    def _init_tables():
        rowf = row.astype(F32)
        dist = jnp.where(causal, (row - col).astype(F32), 0.0)
        for h in range(HEADS):
            lg = RET_LOG_GAMMA[h]
            dr_ref[h] = jnp.where(causal, jnp.exp(dist * lg), 0.0)
            xi_ref[h] = jnp.exp((rowf + 1.0) * lg)
            zeta_ref[h] = jnp.exp((SCAN_CHUNK - 1.0 - rowf) * lg)

    @pl.when(first_tile)
    def _init_state():
        s_ref[...] = s0_ref[...]
        r_ref[...] = r0_ref[...]
        cbuf[0:SUBLANES, :conv_cols] = tail0_ref[...]

    nbuf[...] = _rms(h_ref[...], mixw_ref[...]).astype(BF16)
    cbuf[SUBLANES:SUBLANES + tt, :] = jnp.dot(nbuf[...], wfront_ref[...], preferred_element_type=F32)
    ba = cbuf[SUBLANES:SUBLANES + tt, ba_cols]

    def proj_piece(w_ref, dst, c0):
        cs = slice(c0, c0 + FILL_COLS)
        dst[:, cs] = jnp.dot(nbuf[...], w_ref[:, cs], preferred_element_type=F32)

    deferred = []
    if not meta:
        for w_ref, dst in ((wz_ref, pz), (wrg_ref, prg), (wga_ref, pga), (wgb_ref, pgb)):
            deferred += [functools.partial(proj_piece, w_ref, dst, c0)
                         for c0 in range(0, w_ref.shape[1], FILL_COLS)]
    n_out_gate_pieces = len(deferred) // 2

    def fill():
        if deferred:
            deferred.pop(0)()

    def fill_until(remaining):
        while len(deferred) > remaining:
            fill()

    beta_all = _sigmoid(ba)
    g_all = -jnp.exp(alog_ref[...]) * _softplus(ba + dtb_ref[...])

    def conv_silu(r0, c0):
        cs = slice(c0, c0 + HEAD_DIM)
        hist = cbuf[r0:r0 + SUBLANES + SCAN_CHUNK, cs]
        y = cw_ref[CONV_K - 1:CONV_K, cs] * hist[SUBLANES:]
        for back in range(1, CONV_K):
            tap = CONV_K - 1 - back
            y = y + cw_ref[tap:tap + 1, cs] * pltpu.roll(hist, back, axis=0)[SUBLANES:]
        return y * _sigmoid(y)

    def l2norm(t):
        return t * lax.rsqrt(jnp.sum(t * t, axis=-1, keepdims=True) + EPS)

    def lane_bcast(a, lane):
        return jnp.broadcast_to(a[:, lane:lane + 1], (SCAN_CHUNK, LANES))

    def rotary(t, cos, sin):
        return t * cos + pltpu.roll(t, LANES // 2, axis=1) * sin

    qk_scale = HEAD_DIM ** -0.5
    for c in range(n_chunks):
        r0 = c * SCAN_CHUNK
        rows = slice(r0, r0 + SCAN_CHUNK)
        g_c = g_all[rows]
        g_hi = g_c.astype(BF16)
        g_r1 = g_c - g_hi.astype(F32)
        g_mid = g_r1.astype(BF16)
        g_lo = (g_r1 - g_mid.astype(F32)).astype(BF16)
        cs_dot = functools.partial(jnp.dot, preferred_element_type=F32)
        gc = cs_dot(ltri, g_hi) + (cs_dot(ltri, g_mid) + cs_dot(ltri, g_lo))
        gc_t = gc.T
        beta_c = beta_all[rows]
        cos = cos_ref[rows, :]
        sin = sin_ref[rows, :]

        heads = range(HEADS)
        hsl = [slice(h * HEAD_DIM, (h + 1) * HEAD_DIM) for h in heads]

        q = [l2norm(conv_silu(r0, h * HEAD_DIM)) * qk_scale for h in heads]
        k = [l2norm(conv_silu(r0, (HEADS + h) * HEAD_DIM)) for h in heads]
        v = [conv_silu(r0, (2 * HEADS + h) * HEAD_DIM) for h in heads]
        gcol = [lane_bcast(gc, HEADS + h) for h in heads]
        grow = [gc_t[HEADS + h:HEADS + h + 1, :] for h in heads]
        bcol = [lane_bcast(beta_c, h) for h in heads]
        decay = [jnp.exp(jnp.where(causal, gcol[h] - grow[h], MASKED_LOG)) for h in heads]
        kq = [_dot_nt(jnp.concatenate([k[h], q[h]], axis=0), k[h]) for h in heads]
        crows = slice(SUBLANES + r0, SUBLANES + r0 + SCAN_CHUNK)
        front = lambda c0, h: cbuf[crows, c0 + h * HEAD_DIM:c0 + (h + 1) * HEAD_DIM]
        rq = [rotary(front(rq0, h), cos, sin) for h in heads]
        rk = [rotary(front(rk0, h), cos, sin) * qk_scale for h in heads]
        rv = [front(rv0, h) for h in heads]
        if not meta:
            scores = [_dot_nt(rq[h], rk[h]) * dr_ref[h] for h in heads]
        a_low = [jnp.where(strict, bcol[h] * kq[h][:SCAN_CHUNK] * decay[h], 0.0) for h in heads]
        qk = [kq[h][SCAN_CHUNK:] * decay[h] for h in heads]
        eg = [jnp.exp(gcol[h]) for h in heads]
        uw = _unit_lower_solve(
            a_low, [jnp.concatenate([bcol[h] * v[h], bcol[h] * eg[h] * k[h]], axis=1) for h in heads],
            row, col, fill)
        g_last = [gcol[h][SCAN_CHUNK - 1:SCAN_CHUNK, :] for h in heads]
        k_dec = [k[h] * jnp.exp(g_last[h] - gcol[h]) for h in heads]

        state = [s_ref[h] for h in heads]
        rstate = [r_ref[h] for h in heads]
        w_s = [_dot(uw[h][:, HEAD_DIM:], state[h]) for h in heads]
        rdelta = [_dot_tn(rk[h] * zeta_ref[h], rv[h]) for h in heads]
        v_new = [uw[h][:, :HEAD_DIM] - w_s[h] for h in heads]
        sdelta = [_dot_tn(k_dec[h], v_new[h]) for h in heads]
        for h in heads:
            s_ref[h] = state[h] * jnp.exp(g_last[h]) + sdelta[h]
            r_ref[h] = rstate[h] * math.exp(SCAN_CHUNK * RET_LOG_GAMMA[h]) + rdelta[h]
        if not meta:
            fill_until(n_out_gate_pieces)
            o = [_dot(jnp.concatenate([qk[h], q[h] * eg[h]], axis=1),
                      jnp.concatenate([v_new[h], state[h]], axis=0)) for h in heads]
            ob = [_dot(jnp.concatenate([scores[h], rq[h] * xi_ref[h]], axis=1),
                       jnp.concatenate([rv[h], rstate[h]], axis=0)) for h in heads]
            for h in heads:
                oh = o[h] * lax.rsqrt(jnp.mean(o[h] * o[h], axis=-1, keepdims=True) + EPS) * gn_ref[...]
                z = pz[rows, hsl[h]]
                ya[rows, hsl[h]] = (oh * (z * _sigmoid(z))).astype(BF16)
                mu = jnp.mean(ob[h], axis=-1, keepdims=True)
                cen = ob[h] - mu
                var = jnp.mean(cen * cen, axis=-1, keepdims=True)
                obn = cen * lax.rsqrt(var + EPS) * rn_ref[:, hsl[h]]
                rg = prg[rows, hsl[h]]
                yb[rows, hsl[h]] = (rg * _sigmoid(rg) * obn).astype(BF16)

    cbuf[0:SUBLANES, :conv_cols] = cbuf[tt:tt + SUBLANES, :conv_cols]

    if meta:
        s_out[...] = s_ref[...]
        r_out[...] = r_ref[...]
        tail_out[...] = cbuf[0:SUBLANES, :conv_cols]
    else:
        fill_until(0)
        merged = (_sigmoid(pga[...]) * jnp.dot(ya[...], wbg_ref[...], preferred_element_type=F32)
                  + _sigmoid(pgb[...]) * jnp.dot(yb[...], wbr_ref[...], preferred_element_type=F32))
        out_ref[...] = h_ref[...] + jnp.dot(merged.astype(BF16), wout_ref[...], preferred_element_type=F32)


def _mixer(h2, cos, sin, params, s0, r0, tail0, *, batch, meta):
    rows, d = h2.shape
    seq = rows // batch
    tt = SCAN_CHUNK if meta else MIXER_ROWS
    assert seq % tt == 0
    nt = seq // tt
    qkv_cols = 3 * HEADS * HEAD_DIM
    state_shape = (HEADS, HEAD_DIM, HEAD_DIM)
    in_specs = [
        pl.BlockSpec((tt, d), lambda b, i: (b * nt + i, 0)),
        pl.BlockSpec((tt, LANES), lambda b, i: (i, 0)),
        pl.BlockSpec((tt, LANES), lambda b, i: (i, 0)),
    ] + [_resident(p.shape) for p in params] + [
        _resident(state_shape), _resident(state_shape), _resident((SUBLANES, qkv_cols))]
    if meta:
        out_shape = [jax.ShapeDtypeStruct(state_shape, F32), jax.ShapeDtypeStruct(state_shape, F32),
                     jax.ShapeDtypeStruct((SUBLANES, qkv_cols), F32)]
        out_specs = [pl.BlockSpec(state_shape, lambda b, i: (0, 0, 0)),
                     pl.BlockSpec(state_shape, lambda b, i: (0, 0, 0)),
                     pl.BlockSpec((SUBLANES, qkv_cols), lambda b, i: (0, 0))]
    else:
        out_shape = jax.ShapeDtypeStruct((rows, d), F32)
        out_specs = pl.BlockSpec((tt, d), lambda b, i: (b * nt + i, 0))
    act = lambda dt: pltpu.VMEM((tt, d), dt)
    table = pltpu.VMEM(state_shape, F32)
    front_cols = params[1].shape[1]
    scratch = [pltpu.VMEM((SUBLANES + tt, front_cols), F32), act(BF16),
               act(F32), act(F32), act(F32), act(F32),
               act(BF16), act(BF16), table, table, table, table, table]
    return pl.pallas_call(
        functools.partial(_mixer_body, meta=meta),
        grid=(batch, nt),
        in_specs=in_specs,
        out_specs=out_specs,
        out_shape=out_shape,
        scratch_shapes=scratch,
        compiler_params=pltpu.CompilerParams(
            dimension_semantics=("arbitrary", "arbitrary"),
            vmem_limit_bytes=V7X_VMEM_BYTES * 7 // 8),
        name="mixer_meta" if meta else "mixer",
    )(h2, cos, sin, *params, s0, r0, tail0)


def _deinterleave_heads(w):
    d_in = w.shape[0]
    perm = jnp.concatenate([jnp.arange(0, HEAD_DIM, 2), jnp.arange(1, HEAD_DIM, 2)])
    return w.reshape(d_in, HEADS, HEAD_DIM)[:, :, perm].reshape(d_in, HEADS * HEAD_DIM)


def kernel(x, meta_tokens, ffn1_norm, ffn1_w_in, ffn1_w_out, mix_norm, w_in, gdn_conv_w, gdn_a_log, gdn_dt_bias, gdn_out_norm, ret_out_norm, w_branch_gdn, w_branch_ret, w_out, ffn2_norm, ffn2_w_in, ffn2_w_out, final_norm):
    batch, seq, d = x.shape
    assert ffn1_norm.shape[0] == 1, "single-layer stack only"
    assert meta_tokens.shape[0] == N_META and gdn_conv_w.shape[1] == CONV_K
    hd = HEADS * HEAD_DIM
    row = lambda v: v.reshape(1, -1).astype(F32)

    w = w_in[0]
    bounds = [0, 3 * hd, 4 * hd, 4 * hd + HEADS, 4 * hd + 2 * HEADS]
    for _ in range(6):
        bounds.append(bounds[-1] + hd)
    assert bounds[-1] == w.shape[1]
    piece = lambda j: w[:, bounds[j]:bounds[j + 1]]
    w_ba = jnp.zeros((d, LANES), F32).at[:, :2 * HEADS].set(jnp.concatenate([piece(2), piece(3)], axis=1))
    gate_vec = lambda v: jnp.zeros((1, LANES), F32).at[0, HEADS:2 * HEADS].set(v[0].astype(F32))
    conv_w = jnp.zeros((SUBLANES, 3 * hd), F32).at[:CONV_K].set(gdn_conv_w[0].astype(F32))
    bf = lambda a: a.astype(BF16)
    w_front = jnp.concatenate(
        [piece(0), w_ba, _deinterleave_heads(piece(4)), _deinterleave_heads(piece(5)), piece(6)], axis=1)
    params = (
        row(mix_norm[0]), bf(w_front), bf(piece(1)), bf(piece(7)), bf(piece(8)), bf(piece(9)),
        conv_w, gate_vec(gdn_a_log), gate_vec(gdn_dt_bias), row(gdn_out_norm[0]), row(ret_out_norm[0]),
        bf(w_branch_gdn[0]), bf(w_branch_ret[0]), bf(w_out[0]),
        jnp.tril(jnp.ones((SCAN_CHUNK, SCAN_CHUNK), BF16)),
    )
    ffn1 = (row(ffn1_norm[0]), bf(ffn1_w_in[0]), bf(ffn1_w_out[0]), row(final_norm))
    ffn2 = (row(ffn2_norm[0]), bf(ffn2_w_in[0]), bf(ffn2_w_out[0]), row(final_norm))

    inv = 1.0 / (ROPE_BASE ** jnp.linspace(0.0, 1.0, HEAD_DIM // 2, dtype=F32))
    inv2 = jnp.concatenate([inv, inv]).reshape(1, LANES)

    pad = SCAN_CHUNK - N_META
    meta_block = jnp.concatenate([jnp.zeros((pad, d), F32), meta_tokens.astype(F32)], axis=0)
    cos_m, sin_m = _rope_tables(inv2, SCAN_CHUNK, -pad)
    zero_state = jnp.zeros((HEADS, HEAD_DIM, HEAD_DIM), F32)
    h_meta = _ffn(meta_block, *ffn1, final=False)
    s_m, r_m, tail_m = _mixer(h_meta, cos_m, sin_m, params, zero_state, zero_state,
                              jnp.zeros((SUBLANES, 3 * hd), F32), batch=1, meta=True)

    cos_r, sin_r = _rope_tables(inv2, seq, N_META)
    h = _ffn(x.reshape(batch * seq, d), *ffn1, final=False)
    h = _mixer(h, cos_r, sin_r, params, s_m, r_m, tail_m, batch=batch, meta=False)
    h = _ffn(h, *ffn2, final=True)
    return h.reshape(batch, seq, d)
```

```python
import functools
import math

import jax
import jax.numpy as jnp
from jax import lax
from jax.experimental import pallas as pl
from jax.experimental.pallas import tpu as pltpu

F32 = jnp.float32
BF16 = jnp.bfloat16

EPS = 1e-6
N_META = 16
HEADS = 8
HEAD_DIM = 128
CONV_K = 4
ROPE_BASE = 10000.0
RET_LOG_GAMMA = tuple(math.log1p(-(2.0 ** (-5.0 - h))) for h in range(HEADS))
MASKED_LOG = -1e30

LANES = 128
SUBLANES = 8
V7X_VMEM_BYTES = 64 * 1024 * 1024

SCAN_CHUNK = LANES
SOLVE_BLOCK = 2 * SUBLANES
MIXER_ROWS = 2 * SCAN_CHUNK
FFN_ROWS = 512
FFN_COLS = 256
FILL_COLS = 256
ROPE_ROWS = 256


def _dot(a, b):
    return jnp.dot(a.astype(BF16), b.astype(BF16), preferred_element_type=F32)


def _dot_nt(a, b):
    return lax.dot_general(a.astype(BF16), b.astype(BF16), (((1,), (1,)), ((), ())),
                           preferred_element_type=F32)


def _dot_tn(a, b):
    return lax.dot_general(a.astype(BF16), b.astype(BF16), (((0,), (0,)), ((), ())),
                           preferred_element_type=F32)


def _sigmoid(x):
    return 1.0 / (1.0 + jnp.exp(-x))


def _softplus(x):
    return jnp.maximum(x, 0.0) + jnp.log1p(jnp.exp(-jnp.abs(x)))


def _rms(x, w):
    return x * lax.rsqrt(jnp.mean(x * x, axis=-1, keepdims=True) + EPS) * w


def _rope_body(inv_ref, cos_ref, sin_ref, *, pos0):
    rows = cos_ref.shape[0]
    row = lax.broadcasted_iota(jnp.int32, (rows, LANES), 0)
    lane = lax.broadcasted_iota(jnp.int32, (rows, LANES), 1)
    pos = (row + (pl.program_id(0) * rows + pos0)).astype(F32)
    ang = pos * inv_ref[...]
    cos_ref[...] = jnp.cos(ang)
    sin_ref[...] = jnp.where(lane < LANES // 2, -1.0, 1.0) * jnp.sin(ang)


def _rope_tables(inv2, n_rows, pos0):
    rows = min(ROPE_ROWS, n_rows)
    assert n_rows % rows == 0
    out = jax.ShapeDtypeStruct((n_rows, LANES), F32)
    spec = pl.BlockSpec((rows, LANES), lambda i: (i, 0))
    return pl.pallas_call(
        functools.partial(_rope_body, pos0=pos0),
        grid=(n_rows // rows,),
        in_specs=[pl.BlockSpec((1, LANES), lambda i: (0, 0))],
        out_specs=[spec, spec],
        out_shape=[out, out],
        name="rope_tables",
    )(inv2)


def _ffn_body(x_ref, nw_ref, win_ref, wout_ref, fw_ref, o_ref, *, final):
    d_ff = wout_ref.shape[0]
    x = x_ref[...]
    n = _rms(x, nw_ref[...]).astype(BF16)
    acc = jnp.zeros(x.shape, F32)
    for f in range(0, d_ff, FFN_COLS):
        gate = jnp.dot(n, win_ref[:, f:f + FFN_COLS], preferred_element_type=F32)
        up = jnp.dot(n, win_ref[:, d_ff + f:d_ff + f + FFN_COLS], preferred_element_type=F32)
        act = (gate * _sigmoid(gate) * up).astype(BF16)
        acc = acc + jnp.dot(act, wout_ref[f:f + FFN_COLS, :], preferred_element_type=F32)
    h = x + 0.5 * acc
    if final:
        h = _rms(h, fw_ref[...])
    o_ref[...] = h


def _resident(shape):
    return pl.BlockSpec(shape, lambda *_: (0,) * len(shape), pipeline_mode=pl.Buffered(1))


def _ffn(x2, norm_w, w_in, w_out, final_w, *, final):
    rows, d = x2.shape
    d_ff = w_out.shape[0]
    assert d_ff % FFN_COLS == 0
    tm = min(FFN_ROWS, rows)
    assert rows % tm == 0
    return pl.pallas_call(
        functools.partial(_ffn_body, final=final),
        grid=(rows // tm,),
        in_specs=[
            pl.BlockSpec((tm, d), lambda i: (i, 0)),
            _resident((1, d)),
            _resident((d, 2 * d_ff)),
            _resident((d_ff, d)),
            _resident((1, d)),
        ],
        out_specs=pl.BlockSpec((tm, d), lambda i: (i, 0)),
        out_shape=jax.ShapeDtypeStruct((rows, d), F32),
        compiler_params=pltpu.CompilerParams(
            dimension_semantics=("arbitrary",),
            vmem_limit_bytes=V7X_VMEM_BYTES * 3 // 4),
        name="ffn_final" if final else "ffn",
    )(x2, norm_w, w_in, w_out, final_w)


def _causal_masks():
    row = lax.broadcasted_iota(jnp.int32, (SCAN_CHUNK, SCAN_CHUNK), 0)
    col = lax.broadcasted_iota(jnp.int32, (SCAN_CHUNK, SCAN_CHUNK), 1)
    return row, col


def _unit_lower_solve(ns, rhs, row, col, fill):
    probs = range(len(ns))
    blk = (row // SOLVE_BLOCK) == (col // SOLVE_BLOCK)
    eye = (row == col).astype(F32)
    md = [jnp.where(blk, -n, 0.0) for n in ns]
    lo = [jnp.where(blk, 0.0, n) for n in ns]
    ss = [eye + m for m in md]
    mks = [_dot(m, m) for m in md]
    fill()
    k = 2
    while 2 * k < SOLVE_BLOCK:
        rs = [_dot(jnp.concatenate([s, mk], axis=0), mk) for s, mk in zip(ss, mks)]
        fill()
        ss = [s + r[:SCAN_CHUNK] for s, r in zip(ss, rs)]
        mks = [r[SCAN_CHUNK:] for r in rs]
        k *= 2
    td = [s + _dot(s, mk) for s, mk in zip(ss, mks)]
    fill()
    pb = [_dot(td[p], jnp.concatenate([lo[p], rhs[p]], axis=1)) for p in probs]
    fill()
    pl_b = [x[:, :SCAN_CHUNK].astype(BF16) for x in pb]
    bb = [x[:, SCAN_CHUNK:] for x in pb]
    xs = [[b[:SOLVE_BLOCK]] for b in bb]
    xs_b = [[x[0].astype(BF16)] for x in xs]
    for r0 in range(SOLVE_BLOCK, SCAN_CHUNK, SOLVE_BLOCK):
        rows = slice(r0, r0 + SOLVE_BLOCK)
        for p in probs:
            upd = jnp.dot(pl_b[p][rows, :r0], jnp.concatenate(xs_b[p], axis=0),
                          preferred_element_type=F32)
            xs[p].append(bb[p][rows] - upd)
            xs_b[p].append(xs[p][-1].astype(BF16))
        fill()
    return [jnp.concatenate(x, axis=0) for x in xs]


def _mixer_body(h_ref, cos_ref, sin_ref, mixw_ref, wqkv_ref, wz_ref, wba_ref, wrq_ref,
                wrk_ref, wrv_ref, wrg_ref, wga_ref, wgb_ref, cw_ref, alog_ref, dtb_ref,
                gn_ref, rn_ref, wbg_ref, wbr_ref, wout_ref, ltri_ref, s0_ref, r0_ref, tail0_ref,
                *rest, meta):
    if meta:
        s_out, r_out, tail_out = rest[:3]
        scratch = rest[3:]
    else:
        (out_ref,) = rest[:1]
        scratch = rest[1:]
    (cbuf, nbuf, pz, prq, prk, prv, prg, pga, pgb, ya, yb,
     s_ref, r_ref, dr_ref, xi_ref, zeta_ref) = scratch

    tt = h_ref.shape[0]
    n_chunks = tt // SCAN_CHUNK
    first_tile = pl.program_id(1) == 0
    row, col = _causal_masks()
    causal = row >= col
    strict = row > col
    ltri = ltri_ref[...]

    @pl.when(jnp.logical_and(pl.program_id(0) == 0, first_tile))
    def _init_tables():
        rowf = row.astype(F32)
        dist = jnp.where(causal, (row - col).astype(F32), 0.0)
        for h in range(HEADS):
            lg = RET_LOG_GAMMA[h]
            dr_ref[h] = jnp.where(causal, jnp.exp(dist * lg), 0.0)
            xi_ref[h] = jnp.exp((rowf + 1.0) * lg)
            zeta_ref[h] = jnp.exp((SCAN_CHUNK - 1.0 - rowf) * lg)

    @pl.when(first_tile)
    def _init_state():
        s_ref[...] = s0_ref[...]
        r_ref[...] = r0_ref[...]
        cbuf[0:SUBLANES, :] = tail0_ref[...]

    nbuf[...] = _rms(h_ref[...], mixw_ref[...]).astype(BF16)
    proj = lambda w_ref: jnp.dot(nbuf[...], w_ref[...], preferred_element_type=F32)
    cbuf[SUBLANES:SUBLANES + tt, :] = proj(wqkv_ref)
    ba = proj(wba_ref)
    prq[...] = proj(wrq_ref)
    prk[...] = proj(wrk_ref)
    prv[...] = proj(wrv_ref)

    def proj_piece(w_ref, dst, c0):
        cs = slice(c0, c0 + FILL_COLS)
        dst[:, cs] = jnp.dot(nbuf[...], w_ref[:, cs], preferred_element_type=F32)

    deferred = []
    if not meta:
        for w_ref, dst in ((wz_ref, pz), (wrg_ref, prg), (wga_ref, pga), (wgb_ref, pgb)):
            deferred += [functools.partial(proj_piece, w_ref, dst, c0)
                         for c0 in range(0, w_ref.shape[1], FILL_COLS)]
    n_out_gate_pieces = len(deferred) // 2

    def fill():
        if deferred:
            deferred.pop(0)()

    def fill_until(remaining):
        while len(deferred) > remaining:
            fill()

    beta_all = _sigmoid(ba)
    g_all = -jnp.exp(alog_ref[...]) * _softplus(ba + dtb_ref[...])

    def conv_silu(r0, c0):
        cs = slice(c0, c0 + HEAD_DIM)
        hist = cbuf[r0:r0 + SUBLANES + SCAN_CHUNK, cs]
        y = cw_ref[CONV_K - 1:CONV_K, cs] * hist[SUBLANES:]
        for back in range(1, CONV_K):
            tap = CONV_K - 1 - back
            y = y + cw_ref[tap:tap + 1, cs] * pltpu.roll(hist, back, axis=0)[SUBLANES:]
        return y * _sigmoid(y)

    def l2norm(t, scale=1.0):
        return t * (lax.rsqrt(jnp.sum(t * t, axis=-1, keepdims=True) + EPS) * scale)

    def rotary(t, cos, sin):
        return t * cos + pltpu.roll(t, LANES // 2, axis=1) * sin

    qk_scale = HEAD_DIM ** -0.5
    for c in range(n_chunks):
        r0 = c * SCAN_CHUNK
        rows = slice(r0, r0 + SCAN_CHUNK)
        g_c = g_all[rows]
        g_hi = g_c.astype(BF16)
        g_r1 = g_c - g_hi.astype(F32)
        g_mid = g_r1.astype(BF16)
        g_lo = (g_r1 - g_mid.astype(F32)).astype(BF16)
        cs_dot = functools.partial(jnp.dot, preferred_element_type=F32)
        gc = cs_dot(ltri, g_hi) + (cs_dot(ltri, g_mid) + cs_dot(ltri, g_lo))
        gc_t = gc.T
        beta_c = beta_all[rows]
        cos = cos_ref[rows, :]
        sin = sin_ref[rows, :]

        heads = range(HEADS)
        hsl = [slice(h * HEAD_DIM, (h + 1) * HEAD_DIM) for h in heads]

        q = [l2norm(conv_silu(r0, h * HEAD_DIM), qk_scale) for h in heads]
        k = [l2norm(conv_silu(r0, (HEADS + h) * HEAD_DIM)) for h in heads]
        v = [conv_silu(r0, (2 * HEADS + h) * HEAD_DIM) for h in heads]
        grow = [gc_t[HEADS + h:HEADS + h + 1, :] for h in heads]
        gcol = [jnp.broadcast_to(grow[h], (SCAN_CHUNK, LANES)).T for h in heads]
        beta_t = beta_c.T
        bcol = [jnp.broadcast_to(beta_t[h:h + 1, :], (SCAN_CHUNK, LANES)).T for h in heads]
        decay = [jnp.exp(jnp.where(causal, gcol[h] - grow[h], MASKED_LOG)) for h in heads]
        kq = [_dot_nt(jnp.concatenate([k[h], q[h]], axis=0), k[h]) for h in heads]
        fill()
        rq = [rotary(prq[rows, hsl[h]], cos, sin) for h in heads]
        rk = [rotary(prk[rows, hsl[h]], cos, sin) * qk_scale for h in heads]
        rv = [prv[rows, hsl[h]] for h in heads]
        if not meta:
            scores = [_dot_nt(rq[h], rk[h]) * dr_ref[h] for h in heads]
        a_low = [jnp.where(strict, bcol[h] * kq[h][:SCAN_CHUNK] * decay[h], 0.0) for h in heads]
        qk = [kq[h][SCAN_CHUNK:] * decay[h] for h in heads]
        eg = [jnp.exp(gcol[h]) for h in heads]
        uw = _unit_lower_solve(
            a_low, [jnp.concatenate([bcol[h] * v[h], bcol[h] * eg[h] * k[h]], axis=1) for h in heads],
            row, col, fill)
        g_last = [gcol[h][SCAN_CHUNK - 1:SCAN_CHUNK, :] for h in heads]
        k_dec = [k[h] * jnp.exp(g_last[h] - gcol[h]) for h in heads]

        state = [s_ref[h] for h in heads]
        rstate = [r_ref[h] for h in heads]
        w_s = [_dot(uw[h][:, HEAD_DIM:], state[h]) for h in heads]
        fill()
        rdelta = [_dot_tn(rk[h] * zeta_ref[h], rv[h]) for h in heads]
        v_new = [uw[h][:, :HEAD_DIM] - w_s[h] for h in heads]
        sdelta = [_dot_tn(k_dec[h], v_new[h]) for h in heads]
        for h in heads:
            s_ref[h] = state[h] * jnp.exp(g_last[h]) + sdelta[h]
            r_ref[h] = rstate[h] * math.exp(SCAN_CHUNK * RET_LOG_GAMMA[h]) + rdelta[h]
        if not meta:
            fill_until(n_out_gate_pieces)
            o = [_dot(jnp.concatenate([qk[h], q[h] * eg[h]], axis=1),
                      jnp.concatenate([v_new[h], state[h]], axis=0)) for h in heads]
            ob = [_dot(jnp.concatenate([scores[h], rq[h] * xi_ref[h]], axis=1),
                       jnp.concatenate([rv[h], rstate[h]], axis=0)) for h in heads]
            for h in heads:
                oh = o[h] * lax.rsqrt(jnp.mean(o[h] * o[h], axis=-1, keepdims=True) + EPS) * gn_ref[...]
                z = pz[rows, hsl[h]]
                ya[rows, hsl[h]] = (oh * (z * _sigmoid(z))).astype(BF16)
                mu = jnp.mean(ob[h], axis=-1, keepdims=True)
                cen = ob[h] - mu
                var = jnp.mean(cen * cen, axis=-1, keepdims=True)
                obn = cen * lax.rsqrt(var + EPS) * rn_ref[:, hsl[h]]
                rg = prg[rows, hsl[h]]
                yb[rows, hsl[h]] = (rg * _sigmoid(rg) * obn).astype(BF16)

    cbuf[0:SUBLANES, :] = cbuf[tt:tt + SUBLANES, :]

    if meta:
        s_out[...] = s_ref[...]
        r_out[...] = r_ref[...]
        tail_out[...] = cbuf[0:SUBLANES, :]
    else:
        fill_until(0)
        merged = (_sigmoid(pga[...]) * jnp.dot(ya[...], wbg_ref[...], preferred_element_type=F32)
                  + _sigmoid(pgb[...]) * jnp.dot(yb[...], wbr_ref[...], preferred_element_type=F32))
        out_ref[...] = h_ref[...] + jnp.dot(merged.astype(BF16), wout_ref[...], preferred_element_type=F32)


def _mixer(h2, cos, sin, params, s0, r0, tail0, *, batch, meta):
    rows, d = h2.shape
    seq = rows // batch
    tt = SCAN_CHUNK if meta else MIXER_ROWS
    assert seq % tt == 0
    nt = seq // tt
    qkv_cols = 3 * HEADS * HEAD_DIM
    state_shape = (HEADS, HEAD_DIM, HEAD_DIM)
    in_specs = [
        pl.BlockSpec((tt, d), lambda b, i: (b * nt + i, 0)),
        pl.BlockSpec((tt, LANES), lambda b, i: (i, 0)),
        pl.BlockSpec((tt, LANES), lambda b, i: (i, 0)),
    ] + [_resident(p.shape) for p in params] + [
        _resident(state_shape), _resident(state_shape), _resident((SUBLANES, qkv_cols))]
    if meta:
        out_shape = [jax.ShapeDtypeStruct(state_shape, F32), jax.ShapeDtypeStruct(state_shape, F32),
                     jax.ShapeDtypeStruct((SUBLANES, qkv_cols), F32)]
        out_specs = [pl.BlockSpec(state_shape, lambda b, i: (0, 0, 0)),
                     pl.BlockSpec(state_shape, lambda b, i: (0, 0, 0)),
                     pl.BlockSpec((SUBLANES, qkv_cols), lambda b, i: (0, 0))]
    else:
        out_shape = jax.ShapeDtypeStruct((rows, d), F32)
        out_specs = pl.BlockSpec((tt, d), lambda b, i: (b * nt + i, 0))
    act = lambda dt: pltpu.VMEM((tt, d), dt)
    table = pltpu.VMEM(state_shape, F32)
    scratch = [pltpu.VMEM((SUBLANES + tt, qkv_cols), F32), act(BF16),
               act(F32), act(F32), act(F32), act(F32), act(F32), act(F32), act(F32),
               act(BF16), act(BF16), table, table, table, table, table]
    return pl.pallas_call(
        functools.partial(_mixer_body, meta=meta),
        grid=(batch, nt),
        in_specs=in_specs,
        out_specs=out_specs,
        out_shape=out_shape,
        scratch_shapes=scratch,
        compiler_params=pltpu.CompilerParams(
            dimension_semantics=("arbitrary", "arbitrary"),
            vmem_limit_bytes=V7X_VMEM_BYTES * 7 // 8),
        name="mixer_meta" if meta else "mixer",
    )(h2, cos, sin, *params, s0, r0, tail0)


def _deinterleave_heads(w):
    d_in = w.shape[0]
    perm = jnp.concatenate([jnp.arange(0, HEAD_DIM, 2), jnp.arange(1, HEAD_DIM, 2)])
    return w.reshape(d_in, HEADS, HEAD_DIM)[:, :, perm].reshape(d_in, HEADS * HEAD_DIM)


def kernel(x, meta_tokens, ffn1_norm, ffn1_w_in, ffn1_w_out, mix_norm, w_in, gdn_conv_w, gdn_a_log, gdn_dt_bias, gdn_out_norm, ret_out_norm, w_branch_gdn, w_branch_ret, w_out, ffn2_norm, ffn2_w_in, ffn2_w_out, final_norm):
    batch, seq, d = x.shape
    assert ffn1_norm.shape[0] == 1, "single-layer stack only"
    assert meta_tokens.shape[0] == N_META and gdn_conv_w.shape[1] == CONV_K
    hd = HEADS * HEAD_DIM
    row = lambda v: v.reshape(1, -1).astype(F32)

    w = w_in[0]
    bounds = [0, 3 * hd, 4 * hd, 4 * hd + HEADS, 4 * hd + 2 * HEADS]
    for _ in range(6):
        bounds.append(bounds[-1] + hd)
    assert bounds[-1] == w.shape[1]
    piece = lambda j: w[:, bounds[j]:bounds[j + 1]]
    w_ba = jnp.zeros((d, LANES), F32).at[:, :2 * HEADS].set(jnp.concatenate([piece(2), piece(3)], axis=1))
    gate_vec = lambda v: jnp.zeros((1, LANES), F32).at[0, HEADS:2 * HEADS].set(v[0].astype(F32))
    conv_w = jnp.zeros((SUBLANES, 3 * hd), F32).at[:CONV_K].set(gdn_conv_w[0].astype(F32))
    bf = lambda a: a.astype(BF16)
    params = (
        row(mix_norm[0]), bf(piece(0)), bf(piece(1)), bf(w_ba),
        bf(_deinterleave_heads(piece(4))), bf(_deinterleave_heads(piece(5))), bf(piece(6)), bf(piece(7)),
        bf(piece(8)), bf(piece(9)), conv_w, gate_vec(gdn_a_log), gate_vec(gdn_dt_bias),
        row(gdn_out_norm[0]), row(ret_out_norm[0]),
        bf(w_branch_gdn[0]), bf(w_branch_ret[0]), bf(w_out[0]),
        jnp.tril(jnp.ones((SCAN_CHUNK, SCAN_CHUNK), BF16)),
    )
    ffn1 = (row(ffn1_norm[0]), bf(ffn1_w_in[0]), bf(ffn1_w_out[0]), row(final_norm))
    ffn2 = (row(ffn2_norm[0]), bf(ffn2_w_in[0]), bf(ffn2_w_out[0]), row(final_norm))

    inv = 1.0 / (ROPE_BASE ** jnp.linspace(0.0, 1.0, HEAD_DIM // 2, dtype=F32))
    inv2 = jnp.concatenate([inv, inv]).reshape(1, LANES)

    pad = SCAN_CHUNK - N_META
    meta_block = jnp.concatenate([jnp.zeros((pad, d), F32), meta_tokens.astype(F32)], axis=0)
    cos_m, sin_m = _rope_tables(inv2, SCAN_CHUNK, -pad)
    zero_state = jnp.zeros((HEADS, HEAD_DIM, HEAD_DIM), F32)
    h_meta = _ffn(meta_block, *ffn1, final=False)
    s_m, r_m, tail_m = _mixer(h_meta, cos_m, sin_m, params, zero_state, zero_state,
                              jnp.zeros((SUBLANES, 3 * hd), F32), batch=1, meta=True)

    cos_r, sin_r = _rope_tables(inv2, seq, N_META)
    h = _ffn(x.reshape(batch * seq, d), *ffn1, final=False)
    h = _mixer(h, cos_r, sin_r, params, s_m, r_m, tail_m, batch=batch, meta=False)
    h = _ffn(h, *ffn2, final=True)
    return h.reshape(batch, seq, d)
```

```python
import functools
import math

import jax
import jax.numpy as jnp
from jax import lax
from jax.experimental import pallas as pl
from jax.experimental.pallas import tpu as pltpu

F32 = jnp.float32
BF16 = jnp.bfloat16

EPS = 1e-6
N_META = 16
HEADS = 8
HEAD_DIM = 128
CONV_K = 4
ROPE_BASE = 10000.0
RET_LOG_GAMMA = tuple(math.log1p(-(2.0 ** (-5.0 - h))) for h in range(HEADS))
MASKED_LOG = -1e30

LANES = 128
SUBLANES = 8
V7X_VMEM_BYTES = 64 * 1024 * 1024

SCAN_CHUNK = LANES
SOLVE_BLOCK = 2 * SUBLANES
MIXER_ROWS = 2 * SCAN_CHUNK
FFN_ROWS = 512
FFN_COLS = 256
FILL_COLS = 256
ROPE_ROWS = 256


def _dot(a, b):
    return jnp.dot(a.astype(BF16), b.astype(BF16), preferred_element_type=F32)


def _dot_nt(a, b):
    return lax.dot_general(a.astype(BF16), b.astype(BF16), (((1,), (1,)), ((), ())),
                           preferred_element_type=F32)


def _dot_tn(a, b):
    return lax.dot_general(a.astype(BF16), b.astype(BF16), (((0,), (0,)), ((), ())),
                           preferred_element_type=F32)


def _sigmoid(x):
    return 1.0 / (1.0 + jnp.exp(-x))


def _softplus(x):
    return jnp.maximum(x, 0.0) + jnp.log1p(jnp.exp(-jnp.abs(x)))


def _rms(x, w):
    return x * lax.rsqrt(jnp.mean(x * x, axis=-1, keepdims=True) + EPS) * w


def _rope_body(inv_ref, cos_ref, sin_ref, *, pos0):
    rows = cos_ref.shape[0]
    row = lax.broadcasted_iota(jnp.int32, (rows, LANES), 0)
    lane = lax.broadcasted_iota(jnp.int32, (rows, LANES), 1)
    pos = (row + (pl.program_id(0) * rows + pos0)).astype(F32)
    ang = pos * inv_ref[...]
    cos_ref[...] = jnp.cos(ang)
    sin_ref[...] = jnp.where(lane < LANES // 2, -1.0, 1.0) * jnp.sin(ang)


def _rope_tables(inv2, n_rows, pos0):
    rows = min(ROPE_ROWS, n_rows)
    assert n_rows % rows == 0
    out = jax.ShapeDtypeStruct((n_rows, LANES), F32)
    spec = pl.BlockSpec((rows, LANES), lambda i: (i, 0))
    return pl.pallas_call(
        functools.partial(_rope_body, pos0=pos0),
        grid=(n_rows // rows,),
        in_specs=[pl.BlockSpec((1, LANES), lambda i: (0, 0))],
        out_specs=[spec, spec],
        out_shape=[out, out],
        name="rope_tables",
    )(inv2)


def _ffn_body(x_ref, nw_ref, win_ref, wout_ref, fw_ref, o_ref, *, final):
    d_ff = wout_ref.shape[0]
    x = x_ref[...]
    n = _rms(x, nw_ref[...]).astype(BF16)
    acc = jnp.zeros(x.shape, F32)
    for f in range(0, d_ff, FFN_COLS):
        gate = jnp.dot(n, win_ref[:, f:f + FFN_COLS], preferred_element_type=F32)
        up = jnp.dot(n, win_ref[:, d_ff + f:d_ff + f + FFN_COLS], preferred_element_type=F32)
        act = (gate * _sigmoid(gate) * up).astype(BF16)
        acc = acc + jnp.dot(act, wout_ref[f:f + FFN_COLS, :], preferred_element_type=F32)
    h = x + 0.5 * acc
    if final:
        h = _rms(h, fw_ref[...])
    o_ref[...] = h


def _resident(shape):
    return pl.BlockSpec(shape, lambda *_: (0,) * len(shape), pipeline_mode=pl.Buffered(1))


def _ffn(x2, norm_w, w_in, w_out, final_w, *, final):
    rows, d = x2.shape
    d_ff = w_out.shape[0]
    assert d_ff % FFN_COLS == 0
    tm = min(FFN_ROWS, rows)
    assert rows % tm == 0
    return pl.pallas_call(
        functools.partial(_ffn_body, final=final),
        grid=(rows // tm,),
        in_specs=[
            pl.BlockSpec((tm, d), lambda i: (i, 0)),
            _resident((1, d)),
            _resident((d, 2 * d_ff)),
            _resident((d_ff, d)),
            _resident((1, d)),
        ],
        out_specs=pl.BlockSpec((tm, d), lambda i: (i, 0)),
        out_shape=jax.ShapeDtypeStruct((rows, d), F32),
        compiler_params=pltpu.CompilerParams(
            dimension_semantics=("arbitrary",),
            vmem_limit_bytes=V7X_VMEM_BYTES * 3 // 4),
        name="ffn_final" if final else "ffn",
    )(x2, norm_w, w_in, w_out, final_w)


def _causal_masks():
    row = lax.broadcasted_iota(jnp.int32, (SCAN_CHUNK, SCAN_CHUNK), 0)
    col = lax.broadcasted_iota(jnp.int32, (SCAN_CHUNK, SCAN_CHUNK), 1)
    return row, col


def _unit_lower_solve(ns, rhs, row, col, fill):
    probs = range(len(ns))
    blk = (row // SOLVE_BLOCK) == (col // SOLVE_BLOCK)
    eye = (row == col).astype(F32)
    md = [jnp.where(blk, -n, 0.0) for n in ns]
    lo = [jnp.where(blk, 0.0, n) for n in ns]
    ss = [eye + m for m in md]
    mks = [_dot(m, m) for m in md]
    fill()
    k = 2
    while 2 * k < SOLVE_BLOCK:
        rs = [_dot(jnp.concatenate([s, mk], axis=0), mk) for s, mk in zip(ss, mks)]
        fill()
        ss = [s + r[:SCAN_CHUNK] for s, r in zip(ss, rs)]
        mks = [r[SCAN_CHUNK:] for r in rs]
        k *= 2
    td = [s + _dot(s, mk) for s, mk in zip(ss, mks)]
    fill()
    pb = [_dot(td[p], jnp.concatenate([lo[p], rhs[p]], axis=1)) for p in probs]
    fill()
    pl_b = [x[:, :SCAN_CHUNK].astype(BF16) for x in pb]
    bb = [x[:, SCAN_CHUNK:] for x in pb]
    xs = [[b[:SOLVE_BLOCK]] for b in bb]
    xs_b = [[x[0].astype(BF16)] for x in xs]
    for r0 in range(SOLVE_BLOCK, SCAN_CHUNK, SOLVE_BLOCK):
        rows = slice(r0, r0 + SOLVE_BLOCK)
        for p in probs:
            upd = jnp.dot(pl_b[p][rows, :r0], jnp.concatenate(xs_b[p], axis=0),
                          preferred_element_type=F32)
            xs[p].append(bb[p][rows] - upd)
            xs_b[p].append(xs[p][-1].astype(BF16))
        fill()
    return [jnp.concatenate(x, axis=0) for x in xs]


def _mixer_body(h_ref, cos_ref, sin_ref, mixw_ref, wqkv_ref, wz_ref, wba_ref, wrq_ref,
                wrk_ref, wrv_ref, wrg_ref, wga_ref, wgb_ref, cw_ref, alog_ref, dtb_ref,
                gn_ref, rn_ref, wbg_ref, wbr_ref, wout_ref, ltri_ref, s0_ref, r0_ref, tail0_ref,
                *rest, meta):
    if meta:
        s_out, r_out, tail_out = rest[:3]
        scratch = rest[3:]
    else:
        (out_ref,) = rest[:1]
        scratch = rest[1:]
    (cbuf, nbuf, pz, prq, prk, prv, prg, pga, pgb, ya, yb,
     s_ref, r_ref, dr_ref, xi_ref, zeta_ref) = scratch

    tt = h_ref.shape[0]
    n_chunks = tt // SCAN_CHUNK
    first_tile = pl.program_id(1) == 0
    row, col = _causal_masks()
    causal = row >= col
    strict = row > col
    ltri = ltri_ref[...]

    @pl.when(jnp.logical_and(pl.program_id(0) == 0, first_tile))
    def _init_tables():
        rowf = row.astype(F32)
        dist = jnp.where(causal, (row - col).astype(F32), 0.0)
        for h in range(HEADS):
            lg = RET_LOG_GAMMA[h]
            dr_ref[h] = jnp.where(causal, jnp.exp(dist * lg), 0.0)
            xi_ref[h] = jnp.exp((rowf + 1.0) * lg)
            zeta_ref[h] = jnp.exp((SCAN_CHUNK - 1.0 - rowf) * lg)

    @pl.when(first_tile)
    def _init_state():
        s_ref[...] = s0_ref[...]
        r_ref[...] = r0_ref[...]
        cbuf[0:SUBLANES, :] = tail0_ref[...]

    nbuf[...] = _rms(h_ref[...], mixw_ref[...]).astype(BF16)
    proj = lambda w_ref: jnp.dot(nbuf[...], w_ref[...], preferred_element_type=F32)
    cbuf[SUBLANES:SUBLANES + tt, :] = proj(wqkv_ref)
    ba = proj(wba_ref)
    prq[...] = proj(wrq_ref)
    prk[...] = proj(wrk_ref)
    prv[...] = proj(wrv_ref)

    def proj_piece(w_ref, dst, c0):
        cs = slice(c0, c0 + FILL_COLS)
        dst[:, cs] = jnp.dot(nbuf[...], w_ref[:, cs], preferred_element_type=F32)

    deferred = []
    if not meta:
        for w_ref, dst in ((wz_ref, pz), (wrg_ref, prg), (wga_ref, pga), (wgb_ref, pgb)):
            deferred += [functools.partial(proj_piece, w_ref, dst, c0)
                         for c0 in range(0, w_ref.shape[1], FILL_COLS)]
    n_out_gate_pieces = len(deferred) // 2

    n_pieces = len(deferred)
    reserve = [0]

    def fill():
        if len(deferred) > reserve[0]:
            deferred.pop(0)()

    def fill_until(remaining):
        while len(deferred) > remaining:
            deferred.pop(0)()

    beta_all = _sigmoid(ba)
    g_all = -jnp.exp(alog_ref[...]) * _softplus(ba + dtb_ref[...])

    def conv_silu(r0, c0):
        cs = slice(c0, c0 + HEAD_DIM)
        hist = cbuf[r0:r0 + SUBLANES + SCAN_CHUNK, cs]
        y = cw_ref[CONV_K - 1:CONV_K, cs] * hist[SUBLANES:]
        for back in range(1, CONV_K):
            tap = CONV_K - 1 - back
            y = y + cw_ref[tap:tap + 1, cs] * pltpu.roll(hist, back, axis=0)[SUBLANES:]
        return y * _sigmoid(y)

    def l2norm(t, scale=1.0):
        return t * (lax.rsqrt(jnp.sum(t * t, axis=-1, keepdims=True) + EPS) * scale)

    def rotary(t, cos, sin):
        return t * cos + pltpu.roll(t, LANES // 2, axis=1) * sin

    qk_scale = HEAD_DIM ** -0.5
    for c in range(n_chunks):
        reserve[0] = n_pieces * (n_chunks - 1 - c) // n_chunks
        r0 = c * SCAN_CHUNK
        rows = slice(r0, r0 + SCAN_CHUNK)
        g_c = g_all[rows]
        g_hi = g_c.astype(BF16)
        g_r1 = g_c - g_hi.astype(F32)
        g_mid = g_r1.astype(BF16)
        g_lo = (g_r1 - g_mid.astype(F32)).astype(BF16)
        cs_dot = functools.partial(jnp.dot, preferred_element_type=F32)
        gc = cs_dot(ltri, g_hi) + (cs_dot(ltri, g_mid) + cs_dot(ltri, g_lo))
        gc_t = gc.T
        beta_c = beta_all[rows]
        cos = cos_ref[rows, :]
        sin = sin_ref[rows, :]

        heads = range(HEADS)
        hsl = [slice(h * HEAD_DIM, (h + 1) * HEAD_DIM) for h in heads]

        q = [l2norm(conv_silu(r0, h * HEAD_DIM), qk_scale) for h in heads]
        k = [l2norm(conv_silu(r0, (HEADS + h) * HEAD_DIM)) for h in heads]
        v = [conv_silu(r0, (2 * HEADS + h) * HEAD_DIM) for h in heads]
        grow = [gc_t[HEADS + h:HEADS + h + 1, :] for h in heads]
        gcol = [jnp.broadcast_to(grow[h], (SCAN_CHUNK, LANES)).T for h in heads]
        beta_t = beta_c.T
        bcol = [jnp.broadcast_to(beta_t[h:h + 1, :], (SCAN_CHUNK, LANES)).T for h in heads]
        decay = [jnp.exp(jnp.where(causal, gcol[h] - grow[h], MASKED_LOG)) for h in heads]
        kq = [_dot_nt(jnp.concatenate([k[h], q[h]], axis=0), k[h]) for h in heads]
        fill()
        rq = [rotary(prq[rows, hsl[h]], cos, sin) for h in heads]
        rk = [rotary(prk[rows, hsl[h]], cos, sin) * qk_scale for h in heads]
        rv = [prv[rows, hsl[h]] for h in heads]
        if not meta:
            scores = [_dot_nt(rq[h], rk[h]) * dr_ref[h] for h in heads]
        a_low = [jnp.where(strict, bcol[h] * kq[h][:SCAN_CHUNK] * decay[h], 0.0) for h in heads]
        qk = [kq[h][SCAN_CHUNK:] * decay[h] for h in heads]
        eg = [jnp.exp(gcol[h]) for h in heads]
        uw = _unit_lower_solve(
            a_low, [jnp.concatenate([bcol[h] * v[h], bcol[h] * eg[h] * k[h]], axis=1) for h in heads],
            row, col, fill)
        g_last = [gcol[h][SCAN_CHUNK - 1:SCAN_CHUNK, :] for h in heads]
        k_dec = [k[h] * jnp.exp(g_last[h] - gcol[h]) for h in heads]

        state = [s_ref[h] for h in heads]
        rstate = [r_ref[h] for h in heads]
        w_s = [_dot(uw[h][:, HEAD_DIM:], state[h]) for h in heads]
        fill()
        rdelta = [_dot_tn(rk[h] * zeta_ref[h], rv[h]) for h in heads]
        v_new = [uw[h][:, :HEAD_DIM] - w_s[h] for h in heads]
        sdelta = [_dot_tn(k_dec[h], v_new[h]) for h in heads]
        for h in heads:
            s_ref[h] = state[h] * jnp.exp(g_last[h]) + sdelta[h]
            r_ref[h] = rstate[h] * math.exp(SCAN_CHUNK * RET_LOG_GAMMA[h]) + rdelta[h]
        if not meta:
            fill_until(n_out_gate_pieces)
            o = [_dot(jnp.concatenate([qk[h], q[h] * eg[h]], axis=1),
                      jnp.concatenate([v_new[h], state[h]], axis=0)) for h in heads]
            ob = [_dot(jnp.concatenate([scores[h], rq[h] * xi_ref[h]], axis=1),
                       jnp.concatenate([rv[h], rstate[h]], axis=0)) for h in heads]
            for h in heads:
                oh = o[h] * lax.rsqrt(jnp.mean(o[h] * o[h], axis=-1, keepdims=True) + EPS) * gn_ref[...]
                z = pz[rows, hsl[h]]
                ya[rows, hsl[h]] = (oh * (z * _sigmoid(z))).astype(BF16)
                mu = jnp.mean(ob[h], axis=-1, keepdims=True)
                cen = ob[h] - mu
                var = jnp.mean(cen * cen, axis=-1, keepdims=True)
                obn = cen * lax.rsqrt(var + EPS) * rn_ref[:, hsl[h]]
                rg = prg[rows, hsl[h]]
                yb[rows, hsl[h]] = (rg * _sigmoid(rg) * obn).astype(BF16)

    cbuf[0:SUBLANES, :] = cbuf[tt:tt + SUBLANES, :]

    if meta:
        s_out[...] = s_ref[...]
        r_out[...] = r_ref[...]
        tail_out[...] = cbuf[0:SUBLANES, :]
    else:
        fill_until(0)
        merged = (_sigmoid(pga[...]) * jnp.dot(ya[...], wbg_ref[...], preferred_element_type=F32)
                  + _sigmoid(pgb[...]) * jnp.dot(yb[...], wbr_ref[...], preferred_element_type=F32))
        out_ref[...] = h_ref[...] + jnp.dot(merged.astype(BF16), wout_ref[...], preferred_element_type=F32)


def _mixer(h2, cos, sin, params, s0, r0, tail0, *, batch, meta):
    rows, d = h2.shape
    seq = rows // batch
    tt = SCAN_CHUNK if meta else MIXER_ROWS
    assert seq % tt == 0
    nt = seq // tt
    qkv_cols = 3 * HEADS * HEAD_DIM
    state_shape = (HEADS, HEAD_DIM, HEAD_DIM)
    in_specs = [
        pl.BlockSpec((tt, d), lambda b, i: (b * nt + i, 0)),
        pl.BlockSpec((tt, LANES), lambda b, i: (i, 0)),
        pl.BlockSpec((tt, LANES), lambda b, i: (i, 0)),
    ] + [_resident(p.shape) for p in params] + [
        _resident(state_shape), _resident(state_shape), _resident((SUBLANES, qkv_cols))]
    if meta:
        out_shape = [jax.ShapeDtypeStruct(state_shape, F32), jax.ShapeDtypeStruct(state_shape, F32),
                     jax.ShapeDtypeStruct((SUBLANES, qkv_cols), F32)]
        out_specs = [pl.BlockSpec(state_shape, lambda b, i: (0, 0, 0)),
                     pl.BlockSpec(state_shape, lambda b, i: (0, 0, 0)),
                     pl.BlockSpec((SUBLANES, qkv_cols), lambda b, i: (0, 0))]
    else:
        out_shape = jax.ShapeDtypeStruct((rows, d), F32)
        out_specs = pl.BlockSpec((tt, d), lambda b, i: (b * nt + i, 0))
    act = lambda dt: pltpu.VMEM((tt, d), dt)
    table = pltpu.VMEM(state_shape, F32)
    scratch = [pltpu.VMEM((SUBLANES + tt, qkv_cols), F32), act(BF16),
               act(F32), act(F32), act(F32), act(F32), act(F32), act(F32), act(F32),
               act(BF16), act(BF16), table, table, table, table, table]
    return pl.pallas_call(
        functools.partial(_mixer_body, meta=meta),
        grid=(batch, nt),
        in_specs=in_specs,
        out_specs=out_specs,
        out_shape=out_shape,
        scratch_shapes=scratch,
        compiler_params=pltpu.CompilerParams(
            dimension_semantics=("arbitrary", "arbitrary"),
            vmem_limit_bytes=V7X_VMEM_BYTES * 7 // 8),
        name="mixer_meta" if meta else "mixer",
    )(h2, cos, sin, *params, s0, r0, tail0)


def _deinterleave_heads(w):
    d_in = w.shape[0]
    perm = jnp.concatenate([jnp.arange(0, HEAD_DIM, 2), jnp.arange(1, HEAD_DIM, 2)])
    return w.reshape(d_in, HEADS, HEAD_DIM)[:, :, perm].reshape(d_in, HEADS * HEAD_DIM)


def kernel(x, meta_tokens, ffn1_norm, ffn1_w_in, ffn1_w_out, mix_norm, w_in, gdn_conv_w, gdn_a_log, gdn_dt_bias, gdn_out_norm, ret_out_norm, w_branch_gdn, w_branch_ret, w_out, ffn2_norm, ffn2_w_in, ffn2_w_out, final_norm):
    batch, seq, d = x.shape
    assert ffn1_norm.shape[0] == 1, "single-layer stack only"
    assert meta_tokens.shape[0] == N_META and gdn_conv_w.shape[1] == CONV_K
    hd = HEADS * HEAD_DIM
    row = lambda v: v.reshape(1, -1).astype(F32)

    w = w_in[0]
    bounds = [0, 3 * hd, 4 * hd, 4 * hd + HEADS, 4 * hd + 2 * HEADS]
    for _ in range(6):
        bounds.append(bounds[-1] + hd)
    assert bounds[-1] == w.shape[1]
    piece = lambda j: w[:, bounds[j]:bounds[j + 1]]
    w_ba = jnp.zeros((d, LANES), F32).at[:, :2 * HEADS].set(jnp.concatenate([piece(2), piece(3)], axis=1))
    gate_vec = lambda v: jnp.zeros((1, LANES), F32).at[0, HEADS:2 * HEADS].set(v[0].astype(F32))
    conv_w = jnp.zeros((SUBLANES, 3 * hd), F32).at[:CONV_K].set(gdn_conv_w[0].astype(F32))
    bf = lambda a: a.astype(BF16)
    params = (
        row(mix_norm[0]), bf(piece(0)), bf(piece(1)), bf(w_ba),
        bf(_deinterleave_heads(piece(4))), bf(_deinterleave_heads(piece(5))), bf(piece(6)), bf(piece(7)),
        bf(piece(8)), bf(piece(9)), conv_w, gate_vec(gdn_a_log), gate_vec(gdn_dt_bias),
        row(gdn_out_norm[0]), row(ret_out_norm[0]),
        bf(w_branch_gdn[0]), bf(w_branch_ret[0]), bf(w_out[0]),
        jnp.tril(jnp.ones((SCAN_CHUNK, SCAN_CHUNK), BF16)),
    )
    ffn1 = (row(ffn1_norm[0]), bf(ffn1_w_in[0]), bf(ffn1_w_out[0]), row(final_norm))
    ffn2 = (row(ffn2_norm[0]), bf(ffn2_w_in[0]), bf(ffn2_w_out[0]), row(final_norm))

    inv = 1.0 / (ROPE_BASE ** jnp.linspace(0.0, 1.0, HEAD_DIM // 2, dtype=F32))
    inv2 = jnp.concatenate([inv, inv]).reshape(1, LANES)

    pad = SCAN_CHUNK - N_META
    meta_block = jnp.concatenate([jnp.zeros((pad, d), F32), meta_tokens.astype(F32)], axis=0)
    cos_m, sin_m = _rope_tables(inv2, SCAN_CHUNK, -pad)
    zero_state = jnp.zeros((HEADS, HEAD_DIM, HEAD_DIM), F32)
    h_meta = _ffn(meta_block, *ffn1, final=False)
    s_m, r_m, tail_m = _mixer(h_meta, cos_m, sin_m, params, zero_state, zero_state,
                              jnp.zeros((SUBLANES, 3 * hd), F32), batch=1, meta=True)

    cos_r, sin_r = _rope_tables(inv2, seq, N_META)
    h = _ffn(x.reshape(batch * seq, d), *ffn1, final=False)
    h = _mixer(h, cos_r, sin_r, params, s_m, r_m, tail_m, batch=batch, meta=False)
    h = _ffn(h, *ffn2, final=True)
    return h.reshape(batch, seq, d)
```

```python
import functools
import math

import jax
import jax.numpy as jnp
from jax import lax
from jax.experimental import pallas as pl
from jax.experimental.pallas import tpu as pltpu

F32 = jnp.float32
BF16 = jnp.bfloat16

EPS = 1e-6
N_META = 16
HEADS = 8
HEAD_DIM = 128
CONV_K = 4
ROPE_BASE = 10000.0
RET_LOG_GAMMA = tuple(math.log1p(-(2.0 ** (-5.0 - h))) for h in range(HEADS))
MASKED_LOG = -1e30

LANES = 128
SUBLANES = 8
V7X_VMEM_BYTES = 64 * 1024 * 1024

SCAN_CHUNK = LANES
SOLVE_BLOCK = 2 * SUBLANES
MIXER_ROWS = 2 * SCAN_CHUNK
FFN_ROWS = 1024
FFN_COLS = 256
FILL_COLS = 256
ROPE_ROWS = 256


def _dot(a, b):
    return jnp.dot(a.astype(BF16), b.astype(BF16), preferred_element_type=F32)


def _dot_nt(a, b):
    return lax.dot_general(a.astype(BF16), b.astype(BF16), (((1,), (1,)), ((), ())),
                           preferred_element_type=F32)


def _dot_tn(a, b):
    return lax.dot_general(a.astype(BF16), b.astype(BF16), (((0,), (0,)), ((), ())),
                           preferred_element_type=F32)


def _sigmoid(x):
    return 1.0 / (1.0 + jnp.exp(-x))


def _softplus(x):
    return jnp.maximum(x, 0.0) + jnp.log1p(jnp.exp(-jnp.abs(x)))


def _rms(x, w):
    return x * lax.rsqrt(jnp.mean(x * x, axis=-1, keepdims=True) + EPS) * w


def _rope_body(inv_ref, cos_ref, sin_ref, *, pos0):
    rows = cos_ref.shape[0]
    row = lax.broadcasted_iota(jnp.int32, (rows, LANES), 0)
    lane = lax.broadcasted_iota(jnp.int32, (rows, LANES), 1)
    pos = (row + (pl.program_id(0) * rows + pos0)).astype(F32)
    ang = pos * inv_ref[...]
    cos_ref[...] = jnp.cos(ang)
    sin_ref[...] = jnp.where(lane < LANES // 2, -1.0, 1.0) * jnp.sin(ang)


def _rope_tables(inv2, n_rows, pos0):
    rows = min(ROPE_ROWS, n_rows)
    assert n_rows % rows == 0
    out = jax.ShapeDtypeStruct((n_rows, LANES), F32)
    spec = pl.BlockSpec((rows, LANES), lambda i: (i, 0))
    return pl.pallas_call(
        functools.partial(_rope_body, pos0=pos0),
        grid=(n_rows // rows,),
        in_specs=[pl.BlockSpec((1, LANES), lambda i: (0, 0))],
        out_specs=[spec, spec],
        out_shape=[out, out],
        name="rope_tables",
    )(inv2)


def _ffn_body(x_ref, nw_ref, win_ref, wout_ref, fw_ref, o_ref, *, final):
    d_ff = wout_ref.shape[0]
    x = x_ref[...]
    n = _rms(x, nw_ref[...]).astype(BF16)
    acc = jnp.zeros(x.shape, F32)
    for f in range(0, d_ff, FFN_COLS):
        gate = jnp.dot(n, win_ref[:, f:f + FFN_COLS], preferred_element_type=F32)
        up = jnp.dot(n, win_ref[:, d_ff + f:d_ff + f + FFN_COLS], preferred_element_type=F32)
        act = (gate * _sigmoid(gate) * up).astype(BF16)
        acc = acc + jnp.dot(act, wout_ref[f:f + FFN_COLS, :], preferred_element_type=F32)
    h = x + 0.5 * acc
    if final:
        h = _rms(h, fw_ref[...])
    o_ref[...] = h


def _resident(shape):
    return pl.BlockSpec(shape, lambda *_: (0,) * len(shape), pipeline_mode=pl.Buffered(1))


def _ffn(x2, norm_w, w_in, w_out, final_w, *, final):
    rows, d = x2.shape
    d_ff = w_out.shape[0]
    assert d_ff % FFN_COLS == 0
    tm = min(FFN_ROWS, rows)
    assert rows % tm == 0
    return pl.pallas_call(
        functools.partial(_ffn_body, final=final),
        grid=(rows // tm,),
        in_specs=[
            pl.BlockSpec((tm, d), lambda i: (i, 0)),
            _resident((1, d)),
            _resident((d, 2 * d_ff)),
            _resident((d_ff, d)),
            _resident((1, d)),
        ],
        out_specs=pl.BlockSpec((tm, d), lambda i: (i, 0)),
        out_shape=jax.ShapeDtypeStruct((rows, d), F32),
        compiler_params=pltpu.CompilerParams(
            dimension_semantics=("arbitrary",),
            vmem_limit_bytes=V7X_VMEM_BYTES * 3 // 4),
        name="ffn_final" if final else "ffn",
    )(x2, norm_w, w_in, w_out, final_w)


def _causal_masks():
    row = lax.broadcasted_iota(jnp.int32, (SCAN_CHUNK, SCAN_CHUNK), 0)
    col = lax.broadcasted_iota(jnp.int32, (SCAN_CHUNK, SCAN_CHUNK), 1)
    return row, col


def _unit_lower_solve(ns, rhs, row, col, fill):
    probs = range(len(ns))
    blk = (row // SOLVE_BLOCK) == (col // SOLVE_BLOCK)
    eye = (row == col).astype(F32)
    md = [jnp.where(blk, -n, 0.0) for n in ns]
    lo = [jnp.where(blk, 0.0, n) for n in ns]
    ss = [eye + m for m in md]
    mks = [_dot(m, m) for m in md]
    fill()
    k = 2
    while 2 * k < SOLVE_BLOCK:
        rs = [_dot(jnp.concatenate([s, mk], axis=0), mk) for s, mk in zip(ss, mks)]
        fill()
        ss = [s + r[:SCAN_CHUNK] for s, r in zip(ss, rs)]
        mks = [r[SCAN_CHUNK:] for r in rs]
        k *= 2
    td = [s + _dot(s, mk) for s, mk in zip(ss, mks)]
    fill()
    pb = [_dot(td[p], jnp.concatenate([lo[p], rhs[p]], axis=1)) for p in probs]
    fill()
    pl_b = [x[:, :SCAN_CHUNK].astype(BF16) for x in pb]
    bb = [x[:, SCAN_CHUNK:] for x in pb]
    xs = [[b[:SOLVE_BLOCK]] for b in bb]
    xs_b = [[x[0].astype(BF16)] for x in xs]
    for r0 in range(SOLVE_BLOCK, SCAN_CHUNK, SOLVE_BLOCK):
        rows = slice(r0, r0 + SOLVE_BLOCK)
        for p in probs:
            upd = jnp.dot(pl_b[p][rows, :r0], jnp.concatenate(xs_b[p], axis=0),
                          preferred_element_type=F32)
            xs[p].append(bb[p][rows] - upd)
            xs_b[p].append(xs[p][-1].astype(BF16))
        fill()
    return [jnp.concatenate(x, axis=0) for x in xs]


def _mixer_body(h_ref, cos_ref, sin_ref, mixw_ref, wqkv_ref, wz_ref, wba_ref, wrq_ref,
                wrk_ref, wrv_ref, wrg_ref, wga_ref, wgb_ref, cw_ref, alog_ref, dtb_ref,
                gn_ref, rn_ref, wbg_ref, wbr_ref, wout_ref, ltri_ref, s0_ref, r0_ref, tail0_ref,
                *rest, meta):
    if meta:
        s_out, r_out, tail_out = rest[:3]
        scratch = rest[3:]
    else:
        (out_ref,) = rest[:1]
        scratch = rest[1:]
    (cbuf, nbuf, pz, prq, prk, prv, prg, pga, pgb, ya, yb,
     s_ref, r_ref, dr_ref, xi_ref, zeta_ref) = scratch

    tt = h_ref.shape[0]
    n_chunks = tt // SCAN_CHUNK
    first_tile = pl.program_id(1) == 0
    row, col = _causal_masks()
    causal = row >= col
    strict = row > col
    ltri = ltri_ref[...]

    @pl.when(jnp.logical_and(pl.program_id(0) == 0, first_tile))
    def _init_tables():
        rowf = row.astype(F32)
        dist = jnp.where(causal, (row - col).astype(F32), 0.0)
        for h in range(HEADS):
            lg = RET_LOG_GAMMA[h]
            dr_ref[h] = jnp.where(causal, jnp.exp(dist * lg), 0.0)
            xi_ref[h] = jnp.exp((rowf + 1.0) * lg)
            zeta_ref[h] = jnp.exp((SCAN_CHUNK - 1.0 - rowf) * lg)

    @pl.when(first_tile)
    def _init_state():
        s_ref[...] = s0_ref[...]
        r_ref[...] = r0_ref[...]
        cbuf[0:SUBLANES, :] = tail0_ref[...]

    nbuf[...] = _rms(h_ref[...], mixw_ref[...]).astype(BF16)
    proj = lambda w_ref: jnp.dot(nbuf[...], w_ref[...], preferred_element_type=F32)
    cbuf[SUBLANES:SUBLANES + tt, :] = proj(wqkv_ref)
    ba = proj(wba_ref)
    prq[...] = proj(wrq_ref)
    prk[...] = proj(wrk_ref)
    prv[...] = proj(wrv_ref)

    def proj_piece(w_ref, dst, c0):
        cs = slice(c0, c0 + FILL_COLS)
        dst[:, cs] = jnp.dot(nbuf[...], w_ref[:, cs], preferred_element_type=F32)

    deferred = []
    if not meta:
        for w_ref, dst in ((wz_ref, pz), (wrg_ref, prg), (wga_ref, pga), (wgb_ref, pgb)):
            deferred += [functools.partial(proj_piece, w_ref, dst, c0)
                         for c0 in range(0, w_ref.shape[1], FILL_COLS)]
    n_out_gate_pieces = len(deferred) // 2

    n_pieces = len(deferred)
    reserve = [0]

    def fill():
        if len(deferred) > reserve[0]:
            deferred.pop(0)()

    def fill_until(remaining):
        while len(deferred) > remaining:
            deferred.pop(0)()

    beta_all = _sigmoid(ba)
    g_all = -jnp.exp(alog_ref[...]) * _softplus(ba + dtb_ref[...])

    def conv_silu(r0, c0):
        cs = slice(c0, c0 + HEAD_DIM)
        hist = cbuf[r0:r0 + SUBLANES + SCAN_CHUNK, cs]
        y = cw_ref[CONV_K - 1:CONV_K, cs] * hist[SUBLANES:]
        for back in range(1, CONV_K):
            tap = CONV_K - 1 - back
            y = y + cw_ref[tap:tap + 1, cs] * pltpu.roll(hist, back, axis=0)[SUBLANES:]
        return y * _sigmoid(y)

    def l2norm(t, scale=1.0):
        return t * (lax.rsqrt(jnp.sum(t * t, axis=-1, keepdims=True) + EPS) * scale)

    def rotary(t, cos, sin):
        return t * cos + pltpu.roll(t, LANES // 2, axis=1) * sin

    qk_scale = HEAD_DIM ** -0.5
    for c in range(n_chunks):
        reserve[0] = n_pieces * (n_chunks - 1 - c) // n_chunks
        r0 = c * SCAN_CHUNK
        rows = slice(r0, r0 + SCAN_CHUNK)
        g_c = g_all[rows]
        g_hi = g_c.astype(BF16)
        g_r1 = g_c - g_hi.astype(F32)
        g_mid = g_r1.astype(BF16)
        g_lo = (g_r1 - g_mid.astype(F32)).astype(BF16)
        cs_dot = functools.partial(jnp.dot, preferred_element_type=F32)
        gc = cs_dot(ltri, g_hi) + (cs_dot(ltri, g_mid) + cs_dot(ltri, g_lo))
        gc_t = gc.T
        beta_c = beta_all[rows]
        cos = cos_ref[rows, :]
        sin = sin_ref[rows, :]

        heads = range(HEADS)
        hsl = [slice(h * HEAD_DIM, (h + 1) * HEAD_DIM) for h in heads]

        q = [l2norm(conv_silu(r0, h * HEAD_DIM), qk_scale) for h in heads]
        k = [l2norm(conv_silu(r0, (HEADS + h) * HEAD_DIM)) for h in heads]
        v = [conv_silu(r0, (2 * HEADS + h) * HEAD_DIM) for h in heads]
        grow = [gc_t[HEADS + h:HEADS + h + 1, :] for h in heads]
        gcol = [jnp.broadcast_to(grow[h], (SCAN_CHUNK, LANES)).T for h in heads]
        beta_t = beta_c.T
        bcol = [jnp.broadcast_to(beta_t[h:h + 1, :], (SCAN_CHUNK, LANES)).T for h in heads]
        decay = [jnp.exp(jnp.where(causal, gcol[h] - grow[h], MASKED_LOG)) for h in heads]
        kq = [_dot_nt(jnp.concatenate([k[h], q[h]], axis=0), k[h]) for h in heads]
        fill()
        rq = [rotary(prq[rows, hsl[h]], cos, sin) for h in heads]
        rk = [rotary(prk[rows, hsl[h]], cos, sin) * qk_scale for h in heads]
        rv = [prv[rows, hsl[h]] for h in heads]
        if not meta:
            scores = [_dot_nt(rq[h], rk[h]) * dr_ref[h] for h in heads]
        a_low = [jnp.where(strict, bcol[h] * kq[h][:SCAN_CHUNK] * decay[h], 0.0) for h in heads]
        qk = [kq[h][SCAN_CHUNK:] * decay[h] for h in heads]
        eg = [jnp.exp(gcol[h]) for h in heads]
        uw = _unit_lower_solve(
            a_low, [jnp.concatenate([bcol[h] * v[h], bcol[h] * eg[h] * k[h]], axis=1) for h in heads],
            row, col, fill)
        g_last = [gcol[h][SCAN_CHUNK - 1:SCAN_CHUNK, :] for h in heads]
        k_dec = [k[h] * jnp.exp(g_last[h] - gcol[h]) for h in heads]

        state = [s_ref[h] for h in heads]
        rstate = [r_ref[h] for h in heads]
        w_s = [_dot(uw[h][:, HEAD_DIM:], state[h]) for h in heads]
        fill()
        rdelta = [_dot_tn(rk[h] * zeta_ref[h], rv[h]) for h in heads]
        v_new = [uw[h][:, :HEAD_DIM] - w_s[h] for h in heads]
        sdelta = [_dot_tn(k_dec[h], v_new[h]) for h in heads]
        for h in heads:
            s_ref[h] = state[h] * jnp.exp(g_last[h]) + sdelta[h]
            r_ref[h] = rstate[h] * math.exp(SCAN_CHUNK * RET_LOG_GAMMA[h]) + rdelta[h]
        if not meta:
            fill_until(n_out_gate_pieces)
            o = [_dot(jnp.concatenate([qk[h], q[h] * eg[h]], axis=1),
                      jnp.concatenate([v_new[h], state[h]], axis=0)) for h in heads]
            ob = [_dot(jnp.concatenate([scores[h], rq[h] * xi_ref[h]], axis=1),
                       jnp.concatenate([rv[h], rstate[h]], axis=0)) for h in heads]
            for h in heads:
                oh = o[h] * lax.rsqrt(jnp.mean(o[h] * o[h], axis=-1, keepdims=True) + EPS) * gn_ref[...]
                z = pz[rows, hsl[h]]
                ya[rows, hsl[h]] = (oh * (z * _sigmoid(z))).astype(BF16)
                mu = jnp.mean(ob[h], axis=-1, keepdims=True)
                cen = ob[h] - mu
                var = jnp.mean(cen * cen, axis=-1, keepdims=True)
                obn = cen * lax.rsqrt(var + EPS) * rn_ref[:, hsl[h]]
                rg = prg[rows, hsl[h]]
                yb[rows, hsl[h]] = (rg * _sigmoid(rg) * obn).astype(BF16)

    cbuf[0:SUBLANES, :] = cbuf[tt:tt + SUBLANES, :]

    if meta:
        s_out[...] = s_ref[...]
        r_out[...] = r_ref[...]
        tail_out[...] = cbuf[0:SUBLANES, :]
    else:
        fill_until(0)
        merged = (_sigmoid(pga[...]) * jnp.dot(ya[...], wbg_ref[...], preferred_element_type=F32)
                  + _sigmoid(pgb[...]) * jnp.dot(yb[...], wbr_ref[...], preferred_element_type=F32))
        out_ref[...] = h_ref[...] + jnp.dot(merged.astype(BF16), wout_ref[...], preferred_element_type=F32)


def _mixer(h2, cos, sin, params, s0, r0, tail0, *, batch, meta):
    rows, d = h2.shape
    seq = rows // batch
    tt = SCAN_CHUNK if meta else MIXER_ROWS
    assert seq % tt == 0
    nt = seq // tt
    qkv_cols = 3 * HEADS * HEAD_DIM
    state_shape = (HEADS, HEAD_DIM, HEAD_DIM)
    in_specs = [
        pl.BlockSpec((tt, d), lambda b, i: (b * nt + i, 0)),
        pl.BlockSpec((tt, LANES), lambda b, i: (i, 0)),
        pl.BlockSpec((tt, LANES), lambda b, i: (i, 0)),
    ] + [_resident(p.shape) for p in params] + [
        _resident(state_shape), _resident(state_shape), _resident((SUBLANES, qkv_cols))]
    if meta:
        out_shape = [jax.ShapeDtypeStruct(state_shape, F32), jax.ShapeDtypeStruct(state_shape, F32),
                     jax.ShapeDtypeStruct((SUBLANES, qkv_cols), F32)]
        out_specs = [pl.BlockSpec(state_shape, lambda b, i: (0, 0, 0)),
                     pl.BlockSpec(state_shape, lambda b, i: (0, 0, 0)),
                     pl.BlockSpec((SUBLANES, qkv_cols), lambda b, i: (0, 0))]
    else:
        out_shape = jax.ShapeDtypeStruct((rows, d), F32)
        out_specs = pl.BlockSpec((tt, d), lambda b, i: (b * nt + i, 0))
    act = lambda dt: pltpu.VMEM((tt, d), dt)
    table = pltpu.VMEM(state_shape, F32)
    scratch = [pltpu.VMEM((SUBLANES + tt, qkv_cols), F32), act(BF16),
               act(F32), act(F32), act(F32), act(F32), act(F32), act(F32), act(F32),
               act(BF16), act(BF16), table, table, table, table, table]
    return pl.pallas_call(
        functools.partial(_mixer_body, meta=meta),
        grid=(batch, nt),
        in_specs=in_specs,
        out_specs=out_specs,
        out_shape=out_shape,
        scratch_shapes=scratch,
        compiler_params=pltpu.CompilerParams(
            dimension_semantics=("arbitrary", "arbitrary"),
            vmem_limit_bytes=V7X_VMEM_BYTES * 7 // 8),
        name="mixer_meta" if meta else "mixer",
    )(h2, cos, sin, *params, s0, r0, tail0)


def _deinterleave_heads(w):
    d_in = w.shape[0]
    perm = jnp.concatenate([jnp.arange(0, HEAD_DIM, 2), jnp.arange(1, HEAD_DIM, 2)])
    return w.reshape(d_in, HEADS, HEAD_DIM)[:, :, perm].reshape(d_in, HEADS * HEAD_DIM)


def kernel(x, meta_tokens, ffn1_norm, ffn1_w_in, ffn1_w_out, mix_norm, w_in, gdn_conv_w, gdn_a_log, gdn_dt_bias, gdn_out_norm, ret_out_norm, w_branch_gdn, w_branch_ret, w_out, ffn2_norm, ffn2_w_in, ffn2_w_out, final_norm):
    batch, seq, d = x.shape
    assert ffn1_norm.shape[0] == 1, "single-layer stack only"
    assert meta_tokens.shape[0] == N_META and gdn_conv_w.shape[1] == CONV_K
    hd = HEADS * HEAD_DIM
    row = lambda v: v.reshape(1, -1).astype(F32)

    w = w_in[0]
    bounds = [0, 3 * hd, 4 * hd, 4 * hd + HEADS, 4 * hd + 2 * HEADS]
    for _ in range(6):
        bounds.append(bounds[-1] + hd)
    assert bounds[-1] == w.shape[1]
    piece = lambda j: w[:, bounds[j]:bounds[j + 1]]
    w_ba = jnp.zeros((d, LANES), F32).at[:, :2 * HEADS].set(jnp.concatenate([piece(2), piece(3)], axis=1))
    gate_vec = lambda v: jnp.zeros((1, LANES), F32).at[0, HEADS:2 * HEADS].set(v[0].astype(F32))
    conv_w = jnp.zeros((SUBLANES, 3 * hd), F32).at[:CONV_K].set(gdn_conv_w[0].astype(F32))
    bf = lambda a: a.astype(BF16)
    params = (
        row(mix_norm[0]), bf(piece(0)), bf(piece(1)), bf(w_ba),
        bf(_deinterleave_heads(piece(4))), bf(_deinterleave_heads(piece(5))), bf(piece(6)), bf(piece(7)),
        bf(piece(8)), bf(piece(9)), conv_w, gate_vec(gdn_a_log), gate_vec(gdn_dt_bias),
        row(gdn_out_norm[0]), row(ret_out_norm[0]),
        bf(w_branch_gdn[0]), bf(w_branch_ret[0]), bf(w_out[0]),
        jnp.tril(jnp.ones((SCAN_CHUNK, SCAN_CHUNK), BF16)),
    )
    ffn1 = (row(ffn1_norm[0]), bf(ffn1_w_in[0]), bf(ffn1_w_out[0]), row(final_norm))
    ffn2 = (row(ffn2_norm[0]), bf(ffn2_w_in[0]), bf(ffn2_w_out[0]), row(final_norm))

    inv = 1.0 / (ROPE_BASE ** jnp.linspace(0.0, 1.0, HEAD_DIM // 2, dtype=F32))
    inv2 = jnp.concatenate([inv, inv]).reshape(1, LANES)

    pad = SCAN_CHUNK - N_META
    meta_block = jnp.concatenate([jnp.zeros((pad, d), F32), meta_tokens.astype(F32)], axis=0)
    cos_m, sin_m = _rope_tables(inv2, SCAN_CHUNK, -pad)
    zero_state = jnp.zeros((HEADS, HEAD_DIM, HEAD_DIM), F32)
    h_meta = _ffn(meta_block, *ffn1, final=False)
    s_m, r_m, tail_m = _mixer(h_meta, cos_m, sin_m, params, zero_state, zero_state,
                              jnp.zeros((SUBLANES, 3 * hd), F32), batch=1, meta=True)

    cos_r, sin_r = _rope_tables(inv2, seq, N_META)
    h = _ffn(x.reshape(batch * seq, d), *ffn1, final=False)
    h = _mixer(h, cos_r, sin_r, params, s_m, r_m, tail_m, batch=batch, meta=False)
    h = _ffn(h, *ffn2, final=True)
    return h.reshape(batch, seq, d)
```
